```python
import math
import jax, jax.numpy as jnp
from jax import lax
import numpy as np

D_MODEL = 2048
BATCH = 2
SEQ = 4096
DEPTH = 4
DEC_BATCH = 8
DEC_SEQ = 4
PAST_LEN = 16384
PAGE_SIZE = 128

N_HEADS = 8
HEAD_DIM = 128
N_KV_HEADS = 2
ATT_W = N_HEADS * HEAD_DIM
KV_W = N_KV_HEADS * HEAD_DIM
N_IDX_HEADS = 16
IDX_DIM = 64
TOPK_MAX = 256
CONV_W = D_MODEL // 2
CONV_K = 31
ROPE_THETA = 500000.0
QBLOCK = 128
EPS = 1e-6
IN_WIDTHS = (ATT_W, KV_W, KV_W, N_IDX_HEADS * IDX_DIM, IDX_DIM, N_IDX_HEADS, ATT_W, 2 * CONV_W, CONV_W, 2 * D_MODEL)
N_IN = sum(IN_WIDTHS)

kernel_name = 'dsa_conformer_gated_hybrid_step'

F32 = jnp.float32


def rms_norm(x, g):
    xf = x.astype(F32)
    y = xf * lax.rsqrt(jnp.mean(xf * xf, axis=-1, keepdims=True) + EPS)
    return (y * g.astype(F32)).astype(x.dtype)


def layer_norm(x, g, b):
    xf = x.astype(F32)
    mu = jnp.mean(xf, axis=-1, keepdims=True)
    xc = xf - mu
    y = xc * lax.rsqrt(jnp.mean(xc * xc, axis=-1, keepdims=True) + EPS)
    return (y * g.astype(F32) + b.astype(F32)).astype(x.dtype)


def rope(x, pos):
    rd = x.shape[-1] // 4
    half = rd // 2
    inv = ROPE_THETA ** (-jnp.arange(half, dtype=F32) / half)
    ang = pos.astype(F32)[:, None] * inv[None, :]
    cos = jnp.cos(ang)[:, None, :]
    sin = jnp.sin(ang)[:, None, :]
    xf = x.astype(F32)
    x1 = xf[..., :half]
    x2 = xf[..., half:rd]
    out = jnp.concatenate([x1 * cos - x2 * sin, x2 * cos + x1 * sin, xf[..., rd:]], axis=-1)
    return out.astype(x.dtype)


def split_in(z):
    offs = []
    acc = 0
    for w in IN_WIDTHS[:-1]:
        acc += w
        offs.append(acc)
    return jnp.split(z, offs, axis=-1)


def dsa_attend(q, qi, wi, k, v, ki, q_pos, top_k):
    b, t = q.shape[:2]
    n_keys = k.shape[1]
    dots = jnp.einsum('bthd,bsd->bths', qi.astype(F32), ki.astype(F32)) * (IDX_DIM ** -0.5)
    score = jnp.einsum('bths,bth->bts', jax.nn.relu(dots), wi.astype(F32))
    visible = jnp.arange(n_keys, dtype=jnp.int32)[None, :] <= q_pos[:, None]
    score = jnp.where(visible[None], score, -jnp.inf)
    _, idx = lax.top_k(score, top_k)
    valid = idx <= q_pos[None, :, None]
    take = jax.vmap(lambda rows, ids: rows[ids])
    kg = take(k, idx).astype(F32)
    vg = take(v, idx).astype(F32)
    qg = q.astype(F32).reshape(b, t, N_KV_HEADS, N_HEADS // N_KV_HEADS, HEAD_DIM)
    s = jnp.einsum('btkgd,btnkd->btkgn', qg, kg) * (HEAD_DIM ** -0.5)
    s = jnp.where(valid[:, :, None, None, :], s, -jnp.inf)
    p = jax.nn.softmax(s, axis=-1)
    o = jnp.einsum('btkgn,btnkd->btkgd', p, vg)
    return o.reshape(b, t, ATT_W).astype(q.dtype)


def attend_blocked(q, qi, wi, k, v, ki, top_k):
    b, s = q.shape[:2]
    nb = s // QBLOCK

    def to_blocks(a):
        return jnp.moveaxis(a.reshape((b, nb, QBLOCK) + a.shape[2:]), 1, 0)

    starts = jnp.arange(nb, dtype=jnp.int32) * QBLOCK

    def one(args):
        qb, qib, wib, s0 = args
        pos = s0 + jnp.arange(QBLOCK, dtype=jnp.int32)
        return dsa_attend(qb, qib, wib, k, v, ki, pos, top_k)

    o = lax.map(one, (to_blocks(q), to_blocks(qi), to_blocks(wi), starts))
    return jnp.moveaxis(o, 0, 1).reshape(b, s, ATT_W)


def mixer_layer(x, c, q_pos, top_k, k_past, v_past, ki_past, conv_past,
                norm_g, w_ada, b_ada, w_in, q_norm_g, k_norm_g, idx_k_norm_g,
                conv_w, conv_b, conv_ln_g, conv_ln_b, w_attn_out, w_conv_out, w_out):
    b, t, _ = x.shape
    mod = jax.nn.silu(c) @ w_ada + b_ada
    shift, scale, gate = jnp.split(mod, 3, axis=-1)
    h = rms_norm(x, norm_g) * (1 + scale[:, None]) + shift[:, None]
    zq, zk, zv, zqi, zki, zwi, zga, zglu, zgc, zmerge = split_in(h @ w_in)
    q = rope(rms_norm(zq.reshape(b, t, N_HEADS, HEAD_DIM), q_norm_g), q_pos)
    k = rope(rms_norm(zk.reshape(b, t, N_KV_HEADS, HEAD_DIM), k_norm_g), q_pos)
    v = zv.reshape(b, t, N_KV_HEADS, HEAD_DIM)
    qi = rope(zqi.reshape(b, t, N_IDX_HEADS, IDX_DIM), q_pos)
    ki = rope(rms_norm(zki, idx_k_norm_g)[:, :, None], q_pos)[:, :, 0]
    wi = zwi * (N_IDX_HEADS ** -0.5)
    if k_past is None:
        attn = attend_blocked(q, qi, wi, k, v, ki, top_k)
    else:
        k_all = jnp.concatenate([k_past.astype(k.dtype), k], axis=1)
        v_all = jnp.concatenate([v_past.astype(v.dtype), v], axis=1)
        ki_all = jnp.concatenate([ki_past.astype(ki.dtype), ki], axis=1)
        attn = dsa_attend(q, qi, wi, k_all, v_all, ki_all, q_pos, top_k)
    y_att = (attn * jax.nn.silu(zga)) @ w_attn_out
    za, zb = jnp.split(zglu, 2, axis=-1)
    u = za * jax.nn.sigmoid(zb)
    if conv_past is None:
        conv_past = jnp.zeros((b, CONV_K - 1, CONV_W), u.dtype)
    conv_in = jnp.concatenate([conv_past.astype(u.dtype), u], axis=1)
    new_conv = conv_in[:, -(CONV_K - 1):]
    d = lax.conv_general_dilated(conv_in, conv_w[:, None, :].astype(u.dtype), window_strides=(1,),
                                 padding='VALID', dimension_numbers=('NWC', 'WIO', 'NWC'),
                                 feature_group_count=CONV_W) + conv_b
    d = layer_norm(d, conv_ln_g, conv_ln_b)
    y_conv = (jax.nn.silu(d) * jax.nn.silu(zgc)) @ w_conv_out
    g_att, g_conv = jnp.split(jax.nn.sigmoid(zmerge), 2, axis=-1)
    out = (g_att * y_att + g_conv * y_conv) @ w_out
    x = x + gate[:, None] * out
    return x, k, v, ki, new_conv


def setup_inputs(seed: int = 0) -> dict:
    key = jax.random.key(seed)
    ks = jax.random.split(key, 24)
    n_pages = PAST_LEN // PAGE_SIZE
    n_phys = (DEC_BATCH * n_pages * 5) // 4
    nrm = jax.random.normal
    page_table = jax.random.permutation(ks[8], n_phys)[:DEC_BATCH * n_pages].reshape(DEC_BATCH, n_pages).astype(jnp.int32)
    return {
        'x_prompt': nrm(ks[0], (BATCH, SEQ, D_MODEL), F32),
        'x_sample': nrm(ks[1], (DEC_BATCH, DEC_SEQ, D_MODEL), F32),
        'c_prompt': nrm(ks[2], (BATCH, D_MODEL), F32),
        'c_sample': nrm(ks[3], (DEC_BATCH, D_MODEL), F32),
        'cache_k': nrm(ks[4], (DEPTH, n_phys, PAGE_SIZE, N_KV_HEADS, HEAD_DIM), F32),
        'cache_v': nrm(ks[5], (DEPTH, n_phys, PAGE_SIZE, N_KV_HEADS, HEAD_DIM), F32),
        'cache_kidx': nrm(ks[6], (DEPTH, n_phys, PAGE_SIZE, IDX_DIM), F32),
        'state_conv': 0.5 * nrm(ks[7], (DEPTH, DEC_BATCH, CONV_K - 1, CONV_W), F32),
        'page_table': page_table,
        'norm_g': 1.0 + 0.02 * nrm(ks[9], (DEPTH, D_MODEL), F32),
        'w_ada': 0.5 * D_MODEL ** -0.5 * nrm(ks[10], (DEPTH, D_MODEL, 3 * D_MODEL), F32),
        'b_ada': 0.01 * nrm(ks[11], (DEPTH, 3 * D_MODEL), F32),
        'w_in': D_MODEL ** -0.5 * nrm(ks[12], (DEPTH, D_MODEL, N_IN), F32),
        'q_norm_g': 1.0 + 0.02 * nrm(ks[13], (DEPTH, HEAD_DIM), F32),
        'k_norm_g': 1.0 + 0.02 * nrm(ks[14], (DEPTH, HEAD_DIM), F32),
        'idx_k_norm_g': 1.0 + 0.02 * nrm(ks[15], (DEPTH, IDX_DIM), F32),
        'conv_w': CONV_K ** -0.5 * nrm(ks[16], (DEPTH, CONV_K, CONV_W), F32),
        'conv_b': 0.01 * nrm(ks[17], (DEPTH, CONV_W), F32),
        'conv_ln_g': 1.0 + 0.02 * nrm(ks[18], (DEPTH, CONV_W), F32),
        'conv_ln_b': 0.01 * nrm(ks[19], (DEPTH, CONV_W), F32),
        'w_attn_out': ATT_W ** -0.5 * nrm(ks[20], (DEPTH, ATT_W, D_MODEL), F32),
        'w_conv_out': CONV_W ** -0.5 * nrm(ks[21], (DEPTH, CONV_W, D_MODEL), F32),
        'w_out': D_MODEL ** -0.5 * nrm(ks[22], (DEPTH, D_MODEL, D_MODEL), F32),
    }


def reference(x_prompt, x_sample, c_prompt, c_sample, cache_k, cache_v, cache_kidx, state_conv, page_table,
              norm_g, w_ada, b_ada, w_in, q_norm_g, k_norm_g, idx_k_norm_g,
              conv_w, conv_b, conv_ln_g, conv_ln_b, w_attn_out, w_conv_out, w_out):
    seq = x_prompt.shape[1]
    dec_batch, dec_seq = x_sample.shape[:2]
    n_pages = page_table.shape[1]
    page = cache_k.shape[2]
    past_len = n_pages * page
    pos_p = jnp.arange(seq, dtype=jnp.int32)
    pos_s = past_len + jnp.arange(dec_seq, dtype=jnp.int32)
    topk_p = min(TOPK_MAX, seq // 4)
    topk_s = min(TOPK_MAX, (past_len + dec_seq) // 4)
    xp, xs = x_prompt, x_sample
    kps, vps, kips, cps, kss, vss, kiss, css = [], [], [], [], [], [], [], []
    for l in range(DEPTH):
        lp = (norm_g[l], w_ada[l], b_ada[l], w_in[l], q_norm_g[l], k_norm_g[l], idx_k_norm_g[l],
              conv_w[l], conv_b[l], conv_ln_g[l], conv_ln_b[l], w_attn_out[l], w_conv_out[l], w_out[l])
        xp, kp, vp, kip, cp = mixer_layer(xp, c_prompt, pos_p, topk_p, None, None, None, None, *lp)
        k_past = cache_k[l][page_table].reshape(dec_batch, past_len, N_KV_HEADS, HEAD_DIM)
        v_past = cache_v[l][page_table].reshape(dec_batch, past_len, N_KV_HEADS, HEAD_DIM)
        ki_past = cache_kidx[l][page_table].reshape(dec_batch, past_len, IDX_DIM)
        xs, ks_, vs_, kis_, cs_ = mixer_layer(xs, c_sample, pos_s, topk_s, k_past, v_past, ki_past, state_conv[l], *lp)
        kps.append(kp); vps.append(vp); kips.append(kip); cps.append(cp)
        kss.append(ks_); vss.append(vs_); kiss.append(kis_); css.append(cs_)
    return (xp, xs, jnp.stack(kps), jnp.stack(vps), jnp.stack(kips), jnp.stack(cps),
            jnp.stack(kss), jnp.stack(vss), jnp.stack(kiss), jnp.stack(css))
```

```python
import functools

import jax
import jax.numpy as jnp
from jax import lax
from jax.experimental import pallas as pl
from jax.experimental.pallas import tpu as pltpu

F32 = jnp.float32
BF16 = jnp.bfloat16
I32 = jnp.int32

N_HEADS = 8
HEAD_DIM = 128
N_KV_HEADS = 2
HEADS_PER_KV = N_HEADS // N_KV_HEADS
N_IDX_HEADS = 16
IDX_DIM = 64
TOPK_MAX = 256
CONV_K = 31
ROPE_THETA = 500000.0
EPS = 1e-6

LANE = 128
ATT_W = N_HEADS * HEAD_DIM
KV_W = N_KV_HEADS * HEAD_DIM
QI_W = N_IDX_HEADS * IDX_DIM
CONV_HALO = 32
SAMPLE_ROWS = 16
INT_MIN = -2 ** 31
KEY_NEG_INF = -2139095041
NEG_BIG = -1e30
LOG2E = 1.4426950408889634
VMEM_LIMIT = 56 * 1024 * 1024

_NT = (((1,), (1,)), ((), ()))


def _params(sem):
    return pltpu.CompilerParams(dimension_semantics=sem, vmem_limit_bytes=VMEM_LIMIT)


def _resident(block, imap):
    return pl.BlockSpec(block, imap, pipeline_mode=pl.Buffered(1))


def _silu(x):
    return x * jax.nn.sigmoid(x)


def _mod_kernel(c_ref, w_ref, b_ref, o_ref):
    a = _silu(c_ref[...]).astype(BF16)
    o_ref[0] = jnp.dot(a, w_ref[0].astype(BF16), preferred_element_type=F32) + b_ref[0]


def _modulation(c_all, w_ada, b_ada):
    n_layers, d, n3 = w_ada.shape
    rows = c_all.shape[0]
    tn = next(w for w in (512, 256, LANE, n3) if n3 % w == 0)
    return pl.pallas_call(
        _mod_kernel,
        grid=(n_layers, n3 // tn),
        in_specs=[pl.BlockSpec((rows, d), lambda l, j: (0, 0)),
                  pl.BlockSpec((1, d, tn), lambda l, j: (l, 0, j)),
                  pl.BlockSpec((1, 1, tn), lambda l, j: (l, 0, j))],
        out_specs=pl.BlockSpec((1, rows, tn), lambda l, j: (l, 0, j)),
        out_shape=jax.ShapeDtypeStruct((n_layers, rows, n3), F32),
        compiler_params=_params(("parallel", "parallel")),
        name="adaln_mod",
    )(c_all, w_ada, b_ada.reshape(n_layers, 1, n3))


def _norm_kernel(l_ref, x_ref, g_ref, sc_ref, sh_ref, o_ref):
    x = x_ref[0]
    y = x * lax.rsqrt(jnp.mean(x * x, axis=-1, keepdims=True) + EPS) * g_ref[0]
    o_ref[0] = (y * (1.0 + sc_ref[0]) + sh_ref[0]).astype(BF16)


def _norm(lidx, x, norm_g, mod, mod_rows, tm):
    nb, t, d = x.shape
    mod_imap = lambda part: (lambda b, i, l: (mod_rows(l[0], b), 0, part))
    return pl.pallas_call(
        _norm_kernel,
        grid_spec=pltpu.PrefetchScalarGridSpec(
            num_scalar_prefetch=1, grid=(nb, t // tm),
            in_specs=[pl.BlockSpec((1, tm, d), lambda b, i, l: (b, i, 0)),
                      pl.BlockSpec((1, 1, d), lambda b, i, l: (l[0], 0, 0)),
                      pl.BlockSpec((1, mod.shape[1], d), mod_imap(1)),
                      pl.BlockSpec((1, mod.shape[1], d), mod_imap(0))],
            out_specs=pl.BlockSpec((1, tm, d), lambda b, i, l: (b, i, 0))),
        out_shape=jax.ShapeDtypeStruct((nb, t, d), BF16),
        compiler_params=_params(("parallel", "parallel")),
        name="adaln_rmsnorm",
    )(lidx, x, norm_g, mod, mod)


def _rope(n, c, s1, s2, half):
    return n * c + pltpu.roll(n, LANE - half, 1) * s1 + pltpu.roll(n, half, 1) * s2


def _attn_in_kernel(l_ref, h_ref, w_ref, qg_ref, kg_ref, gi_ref, rc_ref, rs1_ref, rs2_ref,
                    ic_ref, is1_ref, is2_ref,
                    q_ref, k32_ref, kbf_ref, v32_ref, vbf_ref, qi_ref, ki32_ref, kibf_ref, wi_ref, *, v_transposed):
    h = h_ref[0]
    rc, rs1, rs2 = rc_ref[...], rs1_ref[...], rs2_ref[...]
    ic, is1, is2 = ic_ref[...], is1_ref[...], is2_ref[...]

    def proj(c0, width):
        return jnp.dot(h, w_ref[0, :, c0:c0 + width], preferred_element_type=F32)

    def head_norm(z, g):
        return z * lax.rsqrt(jnp.mean(z * z, axis=-1, keepdims=True) + EPS) * g

    scale = HEAD_DIM ** -0.5 * LOG2E
    for pair in range(N_HEADS // 2):
        z2 = proj(pair * 2 * HEAD_DIM, 2 * HEAD_DIM)
        for j in range(2):
            z = z2[:, j * HEAD_DIM:(j + 1) * HEAD_DIM]
            r = _rope(head_norm(z, qg_ref[0]), rc, rs1, rs2, HEAD_DIM // 8)
            q_ref[0, 2 * pair + j] = (r * scale).astype(BF16)
    z2 = proj(ATT_W, KV_W)
    for g in range(N_KV_HEADS):
        z = z2[:, g * HEAD_DIM:(g + 1) * HEAD_DIM]
        r = _rope(head_norm(z, kg_ref[0]), rc, rs1, rs2, HEAD_DIM // 8)
        k32_ref[0, :, g * HEAD_DIM:(g + 1) * HEAD_DIM] = r
        kbf_ref[0, :, g * HEAD_DIM:(g + 1) * HEAD_DIM] = r.astype(BF16)
    z2 = proj(ATT_W + KV_W, KV_W)
    v32_ref[0] = z2
    if v_transposed:
        vbf_ref[0, 0] = z2.T.astype(BF16)
    else:
        vbf_ref[0] = z2.astype(BF16)

    lane = lax.broadcasted_iota(I32, (1, LANE), 1)
    lo = (lane < IDX_DIM).astype(F32)
    o_qi = ATT_W + 2 * KV_W
    for quad in range(N_IDX_HEADS // 4):
        z4 = proj(o_qi + quad * 2 * LANE, 2 * LANE)
        for j in range(2):
            r = _rope(z4[:, j * LANE:(j + 1) * LANE], ic, is1, is2, IDX_DIM // 8)
            qi_ref[0, 4 * quad + 2 * j] = (r * lo).astype(BF16)
            qi_ref[0, 4 * quad + 2 * j + 1] = (pltpu.roll(r, IDX_DIM, 1) * lo).astype(BF16)
    z = proj(o_qi + QI_W, LANE)
    ss = jnp.sum(z * z * lo, axis=-1, keepdims=True) * (1.0 / IDX_DIM)
    n = z * lax.rsqrt(ss + EPS) * gi_ref[0]
    r = _rope(n, ic, is1, is2, IDX_DIM // 8)
    ki32_ref[0] = r[:, :IDX_DIM]
    kibf_ref[0] = r.astype(BF16)
    wi_ref[0] = pltpu.roll(z, LANE - IDX_DIM, 1) * (N_IDX_HEADS ** -0.5 * IDX_DIM ** -0.5)


def _attn_in(lidx, h, w_a, qg, kg, gi, rope_tabs, idx_tabs, tm, v_transposed):
    nb, t, d = h.shape
    na = w_a.shape[-1]
    row = lambda b, i, l: (b, i, 0)
    hm = lambda b, i, l: (b, 0, i, 0)
    lay = lambda b, i, l: (l[0], 0, 0)
    tab = pl.BlockSpec((tm, LANE), lambda b, i, l: (i, 0))
    if v_transposed:
        v_out = (jax.ShapeDtypeStruct((nb, t // tm, KV_W, tm), BF16),
                 pl.BlockSpec((1, 1, KV_W, tm), lambda b, i, l: (b, i, 0, 0)))
    else:
        v_out = (jax.ShapeDtypeStruct((nb, t, KV_W), BF16), pl.BlockSpec((1, tm, KV_W), row))
    outs = [
        (jax.ShapeDtypeStruct((nb, N_HEADS, t, HEAD_DIM), BF16), pl.BlockSpec((1, N_HEADS, tm, HEAD_DIM), hm)),
        (jax.ShapeDtypeStruct((nb, t, KV_W), F32), pl.BlockSpec((1, tm, KV_W), row)),
        (jax.ShapeDtypeStruct((nb, t, KV_W), BF16), pl.BlockSpec((1, tm, KV_W), row)),
        (jax.ShapeDtypeStruct((nb, t, KV_W), F32), pl.BlockSpec((1, tm, KV_W), row)),
        v_out,
        (jax.ShapeDtypeStruct((nb, N_IDX_HEADS, t, LANE), BF16), pl.BlockSpec((1, N_IDX_HEADS, tm, LANE), hm)),
        (jax.ShapeDtypeStruct((nb, t, IDX_DIM), F32), pl.BlockSpec((1, tm, IDX_DIM), row)),
        (jax.ShapeDtypeStruct((nb, t, LANE), BF16), pl.BlockSpec((1, tm, LANE), row)),
        (jax.ShapeDtypeStruct((nb, t, LANE), F32), pl.BlockSpec((1, tm, LANE), row)),
    ]
    return pl.pallas_call(
        functools.partial(_attn_in_kernel, v_transposed=v_transposed),
        grid_spec=pltpu.PrefetchScalarGridSpec(
            num_scalar_prefetch=1, grid=(nb, t // tm),
            in_specs=[pl.BlockSpec((1, tm, d), row),
                      _resident((1, d, na), lay),
                      pl.BlockSpec((1, 1, LANE), lay), pl.BlockSpec((1, 1, LANE), lay),
                      pl.BlockSpec((1, 1, LANE), lay),
                      tab, tab, tab, tab, tab, tab],
            out_specs=[o[1] for o in outs]),
        out_shape=[o[0] for o in outs],
        compiler_params=_params(("parallel", "parallel")),
        name="attn_in_proj",
    )(lidx, h, w_a, qg, kg, gi, *rope_tabs, *idx_tabs)


def _conv_in_kernel(l_ref, h_ref, w_ref, u_ref, sgc_ref, *, cw, bw):
    h = h_ref[0]
    for c0 in range(0, cw, bw):
        za = jnp.dot(h, w_ref[0, :, c0:c0 + bw], preferred_element_type=F32)
        zb = jnp.dot(h, w_ref[0, :, cw + c0:cw + c0 + bw], preferred_element_type=F32)
        u_ref[0, :, c0:c0 + bw] = za * jax.nn.sigmoid(zb)
        zg = jnp.dot(h, w_ref[0, :, 2 * cw + c0:2 * cw + c0 + bw], preferred_element_type=F32)
        sgc_ref[0, :, c0:c0 + bw] = _silu(zg).astype(BF16)


def _conv_in(lidx, h, w_b, tm):
    nb, t, d = h.shape
    cw = w_b.shape[-1] // 3
    row = lambda b, i, l: (b, i, 0)
    return pl.pallas_call(
        functools.partial(_conv_in_kernel, cw=cw, bw=min(256, cw)),
        grid_spec=pltpu.PrefetchScalarGridSpec(
            num_scalar_prefetch=1, grid=(nb, t // tm),
            in_specs=[pl.BlockSpec((1, tm, d), row),
                      _resident((1, d, 3 * cw), lambda b, i, l: (l[0], 0, 0))],
            out_specs=[pl.BlockSpec((1, tm, cw), row), pl.BlockSpec((1, tm, cw), row)]),
        out_shape=[jax.ShapeDtypeStruct((nb, t, cw), F32), jax.ShapeDtypeStruct((nb, t, cw), BF16)],
        compiler_params=_params(("parallel", "parallel")),
        name="conv_in_proj",
    )(lidx, h, w_b)


def _gate_in_kernel(l_ref, h_ref, w_ref, sga_ref, gm_ref, *, bw):
    h = h_ref[0]
    for c0 in range(0, ATT_W, bw):
        z = jnp.dot(h, w_ref[0, :, c0:c0 + bw], preferred_element_type=F32)
        sga_ref[0, :, c0:c0 + bw] = _silu(z).astype(BF16)
    for c0 in range(0, gm_ref.shape[-1], bw):
        z = jnp.dot(h, w_ref[0, :, ATT_W + c0:ATT_W + c0 + bw], preferred_element_type=F32)
        gm_ref[0, :, c0:c0 + bw] = jax.nn.sigmoid(z).astype(BF16)


def _gate_in(lidx, h, w_c, tm):
    nb, t, d = h.shape
    nc = w_c.shape[-1]
    row = lambda b, i, l: (b, i, 0)
    return pl.pallas_call(
        functools.partial(_gate_in_kernel, bw=256),
        grid_spec=pltpu.PrefetchScalarGridSpec(
            num_scalar_prefetch=1, grid=(nb, t // tm),
            in_specs=[pl.BlockSpec((1, tm, d), row),
                      _resident((1, d, nc), lambda b, i, l: (l[0], 0, 0))],
            out_specs=[pl.BlockSpec((1, tm, ATT_W), row), pl.BlockSpec((1, tm, nc - ATT_W), row)]),
        out_shape=[jax.ShapeDtypeStruct((nb, t, ATT_W), BF16), jax.ShapeDtypeStruct((nb, t, nc - ATT_W), BF16)],
        compiler_params=_params(("parallel", "parallel")),
        name="gate_in_proj",
    )(lidx, h, w_c)


def _key_to_float(key):
    b = key ^ ((key >> 31) & jnp.int32(0x7FFFFFFF))
    return lax.bitcast_convert_type(b, F32)


def _kth_largest(count_ge, shape, k):
    kf = jnp.float32(k)

    def thr_of(key):
        return _key_to_float(jnp.maximum(key, jnp.int32(KEY_NEG_INF)))

    zero = jnp.zeros(shape, I32)
    t0 = jnp.where(count_ge(thr_of(zero)) >= kf, zero, jnp.int32(INT_MIN))

    def body(it, t):
        cand = t + jnp.left_shift(jnp.int32(1), 30 - it)
        return jnp.where(count_ge(thr_of(cand)) >= kf, cand, t)

    t = lax.fori_loop(0, 31, body, t0)
    return _key_to_float(jnp.maximum(t, jnp.int32(KEY_NEG_INF + 1)))


def _fold_rows(x, op):
    r, n = x.shape
    if r > 64:
        x = op(x.reshape(r // 64, 64, n), axis=0)
    return op(x.reshape(x.shape[0] // 8, 8, n), axis=0)


def _index_scores(qi, wi, kc, rows):
    sc = jnp.zeros((rows, kc.shape[0]), F32)
    for hg in range(N_IDX_HEADS // 4):
        d = lax.dot_general(qi[hg * 4 * rows:(hg + 1) * 4 * rows], kc, _NT, preferred_element_type=F32)
        for j in range(4):
            hd = hg * 4 + j
            sc = sc + jnp.maximum(d[j * rows:(j + 1) * rows], 0.0) * wi[:, hd:hd + 1]
    return sc


def _softmax_step(s, vc, mask, m_ref, l_ref, acc_ref, rows):
    ck = vc.shape[0]
    s = jnp.where(mask[None], s.reshape(HEADS_PER_KV, rows, ck), NEG_BIG).reshape(HEADS_PER_KV * rows, ck)
    m_prev = m_ref[:, :1]
    m_new = jnp.maximum(m_prev, jnp.max(s, axis=-1, keepdims=True))
    alpha = jnp.exp2(m_prev - m_new)
    p = jnp.exp2(s - m_new)
    l_ref[...] = jnp.broadcast_to(alpha * l_ref[:, :1] + jnp.sum(p, axis=-1, keepdims=True), l_ref.shape)
    acc_ref[...] = alpha * acc_ref[...] + jnp.dot(p.astype(BF16), vc, preferred_element_type=F32)
    m_ref[...] = jnp.broadcast_to(m_new, m_ref.shape)


def _softmax_init(m_ref, l_ref, acc_ref):
    m_ref[...] = jnp.full(m_ref.shape, NEG_BIG, F32)
    l_ref[...] = jnp.zeros(l_ref.shape, F32)
    acc_ref[...] = jnp.zeros(acc_ref.shape, F32)


def _softmax_finish(g, sga_ref, o_ref, l_ref, acc_ref, rows):
    o = acc_ref[...] / l_ref[:, :1]
    for j in range(HEADS_PER_KV):
        c0 = (g * HEADS_PER_KV + j) * HEAD_DIM
        gate = sga_ref[0, :, c0:c0 + HEAD_DIM].astype(F32)
        o_ref[0, :, c0:c0 + HEAD_DIM] = (o[j * rows:(j + 1) * rows] * gate).astype(BF16)


def _prompt_attn_kernel(q_ref, qi_ref, wi_ref, k_ref, vt_ref, ki_ref, sga_ref, o_ref,
                        sc_s, m_s, l_s, acc_s, *, tq, ck, topk):
    row0 = pl.program_id(1) * tq
    nch = (row0 + tq + ck - 1) // ck
    wit = wi_ref[0].T
    kpos = lax.broadcasted_iota(I32, (ck, tq), 0)
    qpos = row0 + lax.broadcasted_iota(I32, (ck, tq), 1)

    def score_chunk(c, carry):
        kc = ki_ref[0, pl.ds(pl.multiple_of(c * ck, ck), ck), :]
        sc = jnp.zeros((ck, tq), F32)
        for hd in range(N_IDX_HEADS):
            d = lax.dot_general(kc, qi_ref[0, hd], _NT, preferred_element_type=F32)
            sc = sc + jnp.maximum(d, 0.0) * wit[hd:hd + 1, :]
        sc_s[c] = jnp.where(kpos + c * ck <= qpos, sc, -jnp.inf)
        return carry

    lax.fori_loop(0, nch, score_chunk, 0)

    def count_ge(thr):
        def body(c, acc):
            return acc + _fold_rows(jnp.where(sc_s[c] >= thr, 1.0, 0.0), jnp.sum)

        acc = lax.fori_loop(0, nch, body, jnp.zeros((8, tq), F32))
        return jnp.sum(acc, axis=0, keepdims=True)

    thr = _kth_largest(count_ge, (1, tq), topk)

    _softmax_init(m_s, l_s, acc_s)

    def attend_chunk(c, carry):
        bias = jnp.where(sc_s[c] >= thr, 0.0, NEG_BIG)
        off = pl.multiple_of(c * ck, ck)

        def qk(g):
            kc = k_ref[0, pl.ds(off, ck), g * HEAD_DIM:(g + 1) * HEAD_DIM]
            qg = q_ref[0, g * HEADS_PER_KV:(g + 1) * HEADS_PER_KV].reshape(HEADS_PER_KV * tq, HEAD_DIM)
            return lax.dot_general(kc, qg, _NT, preferred_element_type=F32)

        logits = [qk(g) for g in range(N_KV_HEADS)]
        for g in range(N_KV_HEADS):
            ps, alphas = [], []
            for j in range(HEADS_PER_KV):
                hd = g * HEADS_PER_KV + j
                s = logits[g][:, j * tq:(j + 1) * tq] + bias
                m_prev = m_s[hd]
                m_new = jnp.maximum(m_prev, jnp.max(_fold_rows(s, jnp.max), axis=0, keepdims=True))
                alpha = jnp.exp2(m_prev - m_new)
                p = jnp.exp2(s - m_new)
                l_s[hd] = alpha * l_s[hd] + jnp.sum(_fold_rows(p, jnp.sum), axis=0, keepdims=True)
                m_s[hd] = m_new
                ps.append(p.astype(BF16))
                alphas.append(alpha)
            vtc = vt_ref[0, c, g * HEAD_DIM:(g + 1) * HEAD_DIM, :]
            pv = jnp.dot(vtc, jnp.concatenate(ps, axis=1), preferred_element_type=F32)
            acc_s[g] = jnp.concatenate(alphas, axis=1) * acc_s[g] + pv
        return carry

    lax.fori_loop(0, nch, attend_chunk, 0)
    for hd in range(N_HEADS):
        g, j = divmod(hd, HEADS_PER_KV)
        o = (acc_s[g, :, j * tq:(j + 1) * tq] / l_s[hd]).T
        gate = sga_ref[0, :, hd * HEAD_DIM:(hd + 1) * HEAD_DIM].astype(F32)
        o_ref[0, :, hd * HEAD_DIM:(hd + 1) * HEAD_DIM] = (o * gate).astype(BF16)


def _prompt_attention(q, qi, wi, kbf, vt, kibf, sga, topk, tq, ck):
    nb, _, t, _ = q.shape
    assert vt.shape == (nb, t // ck, KV_W, ck)
    full = lambda b, i: (b, 0, 0)
    row = lambda b, i: (b, i, 0)
    hm = lambda b, i: (b, 0, i, 0)
    return pl.pallas_call(
        functools.partial(_prompt_attn_kernel, tq=tq, ck=ck, topk=topk),
        grid=(nb, t // tq),
        in_specs=[pl.BlockSpec((1, N_HEADS, tq, HEAD_DIM), hm),
                  pl.BlockSpec((1, N_IDX_HEADS, tq, LANE), hm),
                  pl.BlockSpec((1, tq, LANE), row),
                  pl.BlockSpec((1, t, KV_W), full),
                  pl.BlockSpec((1, t // ck, KV_W, ck), lambda b, i: (b, 0, 0, 0)),
                  pl.BlockSpec((1, t, LANE), full),
                  pl.BlockSpec((1, tq, ATT_W), row)],
        out_specs=pl.BlockSpec((1, tq, ATT_W), row),
        out_shape=jax.ShapeDtypeStruct((nb, t, ATT_W), BF16),
        scratch_shapes=[pltpu.VMEM((t // ck, ck, tq), F32),
                        pltpu.VMEM((N_HEADS, 1, tq), F32),
                        pltpu.VMEM((N_HEADS, 1, tq), F32),
                        pltpu.VMEM((N_KV_HEADS, HEAD_DIM, HEADS_PER_KV * tq), F32)],
        compiler_params=_params(("parallel", "arbitrary")),
        name="prompt_sparse_attention",
    )(q, qi, wi, kbf, vt, kibf, sga)


def _sample_scores_kernel(l_ref, pt_ref, qi_ref, wi_ref, *rest, pages):
    page_refs, o_ref, kbuf = rest[:pages], rest[pages], rest[pages + 1]
    page = page_refs[0].shape[1]
    for p in range(pages):
        kbuf[p * page:(p + 1) * page, :] = page_refs[p][0].astype(BF16)
    qi = qi_ref[0].reshape(N_IDX_HEADS * SAMPLE_ROWS, LANE)[:, :IDX_DIM]
    o_ref[0] = _index_scores(qi, wi_ref[0], kbuf[...], SAMPLE_ROWS)


def _sample_scores(lidx, page_table, qi, wi, cache_kidx, n_phys, pages):
    nb, n_pages = page_table.shape
    page = cache_kidx.shape[1]
    page_spec = lambda p: pl.BlockSpec(
        (1, page, IDX_DIM), lambda b, j, l, pt: (l[0] * n_phys + pt[b, j * pages + p], 0, 0))
    return pl.pallas_call(
        functools.partial(_sample_scores_kernel, pages=pages),
        grid_spec=pltpu.PrefetchScalarGridSpec(
            num_scalar_prefetch=2, grid=(nb, n_pages // pages),
            in_specs=[pl.BlockSpec((1, N_IDX_HEADS, SAMPLE_ROWS, LANE), lambda b, j, l, pt: (b, 0, 0, 0)),
                      pl.BlockSpec((1, SAMPLE_ROWS, LANE), lambda b, j, l, pt: (b, 0, 0))]
                     + [page_spec(p) for p in range(pages)],
            out_specs=pl.BlockSpec((1, SAMPLE_ROWS, pages * page), lambda b, j, l, pt: (b, 0, j)),
            scratch_shapes=[pltpu.VMEM((pages * page, IDX_DIM), BF16)]),
        out_shape=jax.ShapeDtypeStruct((nb, SAMPLE_ROWS, n_pages * page), F32),
        compiler_params=_params(("parallel", "arbitrary")),
        name="sample_index_scores",
    )(lidx, page_table, qi, wi, *([cache_kidx] * pages))


def _sample_threshold_kernel(sc_ref, qi_ref, wi_ref, kin_ref, thr_ref, newsc_ref, *, topk, n_new):
    rows = SAMPLE_ROWS
    qi = qi_ref[0].reshape(N_IDX_HEADS * rows, LANE)
    sc_new = _index_scores(qi, wi_ref[0], kin_ref[0], rows)
    r = lax.broadcasted_iota(I32, sc_new.shape, 0)
    c = lax.broadcasted_iota(I32, sc_new.shape, 1)
    sc_new = jnp.where((c <= r) & (c < n_new), sc_new, -jnp.inf)
    newsc_ref[0] = sc_new

    def count_ge(thr):
        tb = jnp.broadcast_to(thr, (rows, LANE))
        accs = [jnp.where(sc_new >= tb, 1.0, 0.0)] + [jnp.zeros((rows, LANE), F32)] * 7
        for j in range(sc_ref.shape[-1] // LANE):
            accs[j % 8] = accs[j % 8] + jnp.where(sc_ref[0, :, j * LANE:(j + 1) * LANE] >= tb, 1.0, 0.0)
        acc = (accs[0] + accs[1]) + (accs[2] + accs[3]) + ((accs[4] + accs[5]) + (accs[6] + accs[7]))
        return jnp.sum(acc, axis=-1, keepdims=True)

    thr_ref[0] = jnp.broadcast_to(_kth_largest(count_ge, (rows, 1), topk), (rows, LANE))


def _sample_threshold(scores, qi, wi, ki_new, topk, n_new):
    nb, rows, n_keys = scores.shape
    b3 = lambda b: (b, 0, 0)
    return pl.pallas_call(
        functools.partial(_sample_threshold_kernel, topk=topk, n_new=n_new),
        grid=(nb,),
        in_specs=[pl.BlockSpec((1, rows, n_keys), b3),
                  pl.BlockSpec((1, N_IDX_HEADS, rows, LANE), lambda b: (b, 0, 0, 0)),
                  pl.BlockSpec((1, rows, LANE), b3),
                  pl.BlockSpec((1, LANE, LANE), b3)],
        out_specs=[pl.BlockSpec((1, rows, LANE), b3), pl.BlockSpec((1, rows, LANE), b3)],
        out_shape=[jax.ShapeDtypeStruct((nb, rows, LANE), F32), jax.ShapeDtypeStruct((nb, rows, LANE), F32)],
        compiler_params=_params(("parallel",)),
        name="sample_topk_threshold",
    )(scores, qi, wi, ki_new)


def _sample_attn_kernel(l_ref, pt_ref, q_ref, sc_ref, thr_ref, newsc_ref, knew_ref, vnew_ref, sga_ref,
                        *rest, pages):
    k_refs, v_refs = rest[:pages], rest[pages:2 * pages]
    o_ref, kbuf, vbuf, m_s, l_s, acc_s = rest[2 * pages:]
    rows = SAMPLE_ROWS
    page = k_refs[0].shape[1] // N_KV_HEADS
    j = pl.program_id(1)
    thr = thr_ref[0][:, :1]

    @pl.when(j == 0)
    def _():
        for g in range(N_KV_HEADS):
            _softmax_init(m_s.at[g], l_s.at[g], acc_s.at[g])

    def q_group(g):
        return q_ref[0, g * HEADS_PER_KV:(g + 1) * HEADS_PER_KV].reshape(HEADS_PER_KV * rows, HEAD_DIM)

    for g in range(N_KV_HEADS):
        for p in range(pages):
            kbuf[g, p * page:(p + 1) * page, :] = k_refs[p][0, pl.ds(g, page, stride=N_KV_HEADS), :].astype(BF16)
            vbuf[g, p * page:(p + 1) * page, :] = v_refs[p][0, pl.ds(g, page, stride=N_KV_HEADS), :].astype(BF16)
    mask = sc_ref[0] >= thr
    logits = [lax.dot_general(q_group(g), kbuf[g], _NT, preferred_element_type=F32) for g in range(N_KV_HEADS)]
    for g in range(N_KV_HEADS):
        _softmax_step(logits[g], vbuf[g], mask, m_s.at[g], l_s.at[g], acc_s.at[g], rows)

    @pl.when(j == pl.num_programs(1) - 1)
    def _():
        mask_new = newsc_ref[0] >= thr
        for g in range(N_KV_HEADS):
            kc = knew_ref[0, :, g * HEAD_DIM:(g + 1) * HEAD_DIM]
            vc = vnew_ref[0, :, g * HEAD_DIM:(g + 1) * HEAD_DIM]
            s = lax.dot_general(q_group(g), kc, _NT, preferred_element_type=F32)
            _softmax_step(s, vc, mask_new, m_s.at[g], l_s.at[g], acc_s.at[g], rows)
            _softmax_finish(g, sga_ref, o_ref, l_s.at[g], acc_s.at[g], rows)


def _sample_attention(lidx, page_table, q, scores, thr, newsc, k_new, v_new, sga, cache_k, cache_v, n_phys, pages):
    nb, n_pages = page_table.shape
    page = cache_k.shape[1] // N_KV_HEADS
    rows = SAMPLE_ROWS
    b3 = lambda b, j, l, pt: (b, 0, 0)
    page_spec = lambda p: pl.BlockSpec(
        (1, page * N_KV_HEADS, HEAD_DIM), lambda b, j, l, pt: (l[0] * n_phys + pt[b, j * pages + p], 0, 0))
    return pl.pallas_call(
        functools.partial(_sample_attn_kernel, pages=pages),
        grid_spec=pltpu.PrefetchScalarGridSpec(
            num_scalar_prefetch=2, grid=(nb, n_pages // pages),
            in_specs=[pl.BlockSpec((1, N_HEADS, rows, HEAD_DIM), lambda b, j, l, pt: (b, 0, 0, 0)),
                      pl.BlockSpec((1, rows, pages * page), lambda b, j, l, pt: (b, 0, j)),
                      pl.BlockSpec((1, rows, LANE), b3), pl.BlockSpec((1, rows, LANE), b3),
                      pl.BlockSpec((1, LANE, KV_W), b3), pl.BlockSpec((1, LANE, KV_W), b3),
                      pl.BlockSpec((1, rows, ATT_W), b3)]
                     + [page_spec(p) for p in range(pages)] * 2,
            out_specs=pl.BlockSpec((1, rows, ATT_W), b3),
            scratch_shapes=[pltpu.VMEM((N_KV_HEADS, pages * page, HEAD_DIM), BF16),
                            pltpu.VMEM((N_KV_HEADS, pages * page, HEAD_DIM), BF16),
                            pltpu.VMEM((N_KV_HEADS, HEADS_PER_KV * rows, LANE), F32),
                            pltpu.VMEM((N_KV_HEADS, HEADS_PER_KV * rows, LANE), F32),
                            pltpu.VMEM((N_KV_HEADS, HEADS_PER_KV * rows, HEAD_DIM), F32)]),
        out_shape=jax.ShapeDtypeStruct((nb, rows, ATT_W), BF16),
        compiler_params=_params(("parallel", "arbitrary")),
        name="sample_sparse_attention",
    )(lidx, page_table, q, scores, thr, newsc, k_new, v_new, sga, *([cache_k] * pages), *([cache_v] * pages))


def _conv_kernel(l_ref, u_ref, st_ref, w_ref, b_ref, g_ref, beta_ref, sgc_ref, o_ref, ext_s, d_s, *, tt, rb, cb):
    cw = u_ref.shape[-1]

    @pl.when(pl.program_id(1) == 0)
    def _():
        ext_s[0:CONV_HALO] = st_ref[0]

    ext_s[CONV_HALO:CONV_HALO + tt] = u_ref[0]
    first = CONV_HALO - (CONV_K - 1)
    for c0 in range(0, cw, cb):
        for r0 in range(0, tt, rb):
            acc = jnp.zeros((rb, cb), F32)
            for j in range(CONV_K):
                acc = acc + ext_s[r0 + first + j:r0 + first + j + rb, c0:c0 + cb] * w_ref[0, j:j + 1, c0:c0 + cb]
            d_s[r0:r0 + rb, c0:c0 + cb] = acc
    d = d_s[...] + b_ref[0]
    mu = jnp.mean(d, axis=-1, keepdims=True)
    xc = d - mu
    y = xc * lax.rsqrt(jnp.mean(xc * xc, axis=-1, keepdims=True) + EPS) * g_ref[0] + beta_ref[0]
    o_ref[0] = (_silu(y) * sgc_ref[0].astype(F32)).astype(BF16)
    if tt >= CONV_HALO:
        ext_s[0:CONV_HALO] = ext_s[tt:tt + CONV_HALO]


def _conv(lidx, u, state, state_rows, conv_w, conv_b, ln_g, ln_b, sgc, tt):
    nb, t, cw = u.shape
    row = lambda b, i, l: (b, i, 0)
    lay = lambda b, i, l: (l[0], 0, 0)
    rb = min(32, tt)
    return pl.pallas_call(
        functools.partial(_conv_kernel, tt=tt, rb=rb, cb=min(256, cw)),
        grid_spec=pltpu.PrefetchScalarGridSpec(
            num_scalar_prefetch=1, grid=(nb, t // tt),
            in_specs=[pl.BlockSpec((1, tt, cw), row),
                      pl.BlockSpec((1, CONV_HALO, cw), lambda b, i, l: (state_rows(l[0], b), 0, 0)),
                      pl.BlockSpec((1, CONV_HALO, cw), lay),
                      pl.BlockSpec((1, 1, cw), lay), pl.BlockSpec((1, 1, cw), lay), pl.BlockSpec((1, 1, cw), lay),
                      pl.BlockSpec((1, tt, cw), row)],
            out_specs=pl.BlockSpec((1, tt, cw), row),
            scratch_shapes=[pltpu.VMEM((CONV_HALO + tt, cw), F32), pltpu.VMEM((tt, cw), F32)]),
        out_shape=jax.ShapeDtypeStruct((nb, t, cw), BF16),
        compiler_params=_params(("parallel", "arbitrary")),
        name="conformer_conv",
    )(lidx, u, state, conv_w, conv_b, ln_g, ln_b, sgc)


def _out_kernel(l_ref, a_ref, cv_ref, gm_ref, x_ref, gate_ref, wa_ref, wc_ref, wo_ref, o_ref):
    d = x_ref.shape[-1]
    y_att = jnp.dot(a_ref[0], wa_ref[0], preferred_element_type=F32)
    y_conv = jnp.dot(cv_ref[0], wc_ref[0], preferred_element_type=F32)
    merged = gm_ref[0, :, :d].astype(F32) * y_att + gm_ref[0, :, d:].astype(F32) * y_conv
    out = jnp.dot(merged.astype(BF16), wo_ref[0], preferred_element_type=F32)
    o_ref[0] = x_ref[0] + gate_ref[0] * out


def _out(lidx, a, cv, gm, x, mod, mod_rows, w_ao, w_co, w_o, tm):
    nb, t, d = x.shape
    cw = cv.shape[-1]
    row = lambda b, i, l: (b, i, 0)
    lay = lambda b, i, l: (l[0], 0, 0)
    return pl.pallas_call(
        _out_kernel,
        grid_spec=pltpu.PrefetchScalarGridSpec(
            num_scalar_prefetch=1, grid=(nb, t // tm),
            in_specs=[pl.BlockSpec((1, tm, ATT_W), row), pl.BlockSpec((1, tm, cw), row),
                      pl.BlockSpec((1, tm, 2 * d), row), pl.BlockSpec((1, tm, d), row),
                      pl.BlockSpec((1, mod.shape[1], d), lambda b, i, l: (mod_rows(l[0], b), 0, 2)),
                      _resident((1, ATT_W, d), lay), _resident((1, cw, d), lay), _resident((1, d, d), lay)],
            out_specs=pl.BlockSpec((1, tm, d), row)),
        out_shape=jax.ShapeDtypeStruct((nb, t, d), F32),
        compiler_params=_params(("parallel", "parallel")),
        name="merge_out_proj",
    )(lidx, a, cv, gm, x, mod, w_ao, w_co, w_o)


def _rope_tables(pos, head_dim):
    rd = head_dim // 4
    half = rd // 2
    inv = ROPE_THETA ** (-jnp.arange(half, dtype=F32) / half)
    ang = pos.astype(F32)[:, None] * inv[None, :]
    cos, sin = jnp.cos(ang), jnp.sin(ang)
    n = pos.shape[0]
    pad = lambda w: jnp.zeros((n, w), F32)
    c = jnp.concatenate([cos, cos, jnp.ones((n, head_dim - rd), F32)], axis=-1)
    s1 = jnp.concatenate([-sin, pad(head_dim - half)], axis=-1)
    s2 = jnp.concatenate([pad(half), sin, pad(head_dim - rd)], axis=-1)
    rep = LANE // head_dim
    return tuple(jnp.tile(a, (1, rep)) for a in (c, s1, s2))


def _pad_rows(a, rows, axis):
    pad = [(0, 0)] * a.ndim
    pad[axis] = (0, rows - a.shape[axis])
    return jnp.pad(a, pad)


def kernel(x_prompt, x_sample, c_prompt, c_sample, cache_k, cache_v, cache_kidx, state_conv, page_table, norm_g, w_ada, b_ada, w_in, q_norm_g, k_norm_g, idx_k_norm_g, conv_w, conv_b, conv_ln_g, conv_ln_b, w_attn_out, w_conv_out, w_out):
    batch, seq, d = x_prompt.shape
    dec_batch, dec_seq, _ = x_sample.shape
    n_layers, n_phys, page = cache_k.shape[:3]
    n_pages = page_table.shape[1]
    past_len = n_pages * page
    cw = conv_w.shape[-1]
    topk_p = min(TOPK_MAX, seq // 4)
    topk_s = min(TOPK_MAX, (past_len + dec_seq) // 4)
    n_dec = dec_batch * dec_seq
    assert dec_seq <= SAMPLE_ROWS and CONV_K - 1 <= CONV_HALO

    o_ki = ATT_W + 2 * KV_W + QI_W
    o_ga = o_ki + IDX_DIM + N_IDX_HEADS
    o_glu = o_ga + ATT_W
    o_mg = o_glu + 3 * cw
    w_a = jnp.concatenate([w_in[..., :o_ga].astype(BF16),
                           jnp.zeros((n_layers, d, LANE - IDX_DIM - N_IDX_HEADS), BF16)], axis=-1)
    w_b = w_in[..., o_glu:o_mg].astype(BF16)
    w_c = jnp.concatenate([w_in[..., o_ga:o_glu], w_in[..., o_mg:]], axis=-1).astype(BF16)
    w_ao, w_co, w_o = w_attn_out.astype(BF16), w_conv_out.astype(BF16), w_out.astype(BF16)
    qg = q_norm_g.reshape(n_layers, 1, HEAD_DIM)
    kg = k_norm_g.reshape(n_layers, 1, HEAD_DIM)
    gi = _pad_rows(idx_k_norm_g, LANE, 1).reshape(n_layers, 1, LANE)
    conv_w_p = _pad_rows(conv_w, CONV_HALO, 1)
    as_row = lambda a: a.reshape(n_layers, 1, a.shape[-1])

    n_c = batch + dec_batch
    c_all = _pad_rows(jnp.concatenate([c_prompt, c_sample], axis=0), -(-n_c // 8) * 8, 0)
    mod = _modulation(c_all, w_ada, b_ada)
    mod_p = mod[:, :batch].reshape(n_layers * batch, 1, 3 * d)
    mod_s = jnp.repeat(mod[:, batch:n_c], dec_seq, axis=1)
    mod_rows_p = lambda l, b: l * batch + b
    mod_rows_s = lambda l, b: l

    rope_p = _rope_tables(jnp.arange(seq, dtype=jnp.int32), HEAD_DIM)
    idx_p = _rope_tables(jnp.arange(seq, dtype=jnp.int32), IDX_DIM)
    pos_s = jnp.tile(past_len + jnp.arange(dec_seq, dtype=jnp.int32), dec_batch)
    rope_s = _rope_tables(pos_s, HEAD_DIM)
    idx_s = _rope_tables(pos_s, IDX_DIM)

    cache_k2 = cache_k.reshape(n_layers * n_phys, page * N_KV_HEADS, HEAD_DIM)
    cache_v2 = cache_v.reshape(n_layers * n_phys, page * N_KV_HEADS, HEAD_DIM)
    cache_ki2 = cache_kidx.reshape(n_layers * n_phys, page, IDX_DIM)
    state_p = jnp.zeros((1, CONV_HALO, cw), F32)
    state_s = jnp.pad(state_conv, ((0, 0), (0, 0), (CONV_HALO - (CONV_K - 1), 0), (0, 0)))
    state_s = state_s.reshape(n_layers * dec_batch, CONV_HALO, cw)

    tm = min(512, seq)
    tq = min(256, seq)
    ck = tm
    tt = min(256, seq)
    pages = min(16, n_pages)

    def to_sample_rows(a, axis):
        shp = a.shape[:axis] + (dec_batch, dec_seq) + a.shape[axis + 1:]
        a = jnp.moveaxis(a.reshape(shp), axis, 0)
        return _pad_rows(a, SAMPLE_ROWS, axis + 1)

    def layer(carry, l):
        xp, xs = carry
        lidx = jnp.reshape(l, (1,)).astype(jnp.int32)

        h = _norm(lidx, xp, as_row(norm_g), mod_p, mod_rows_p, tm)
        q, k32, kbf, v32, vt, qi, ki32, kibf, wi = _attn_in(lidx, h, w_a, qg, kg, gi, rope_p, idx_p, tm, True)
        u, sgc = _conv_in(lidx, h, w_b, tm)
        sga, gm = _gate_in(lidx, h, w_c, tm)
        a = _prompt_attention(q, qi, wi, kbf, vt, kibf, sga, topk_p, tq, ck)
        cv = _conv(lidx, u, state_p, lambda l_, b: 0, conv_w_p, as_row(conv_b), as_row(conv_ln_g),
                   as_row(conv_ln_b), sgc, tt)
        xp_new = _out(lidx, a, cv, gm, xp, mod_p, mod_rows_p, w_ao, w_co, w_o, tm)
        outs_p = (k32.reshape(batch, seq, N_KV_HEADS, HEAD_DIM), v32.reshape(batch, seq, N_KV_HEADS, HEAD_DIM),
                  ki32, u[:, seq - (CONV_K - 1):])

        xs2 = xs.reshape(1, n_dec, d)
        h = _norm(lidx, xs2, as_row(norm_g), mod_s, mod_rows_s, n_dec)
        q, k32, kbf, v32, vbf, qi, ki32, kibf, wi = _attn_in(lidx, h, w_a, qg, kg, gi, rope_s, idx_s, n_dec, False)
        u, sgc = _conv_in(lidx, h, w_b, n_dec)
        sga, gm = _gate_in(lidx, h, w_c, n_dec)
        q_r = to_sample_rows(q[0], 1)
        qi_r = to_sample_rows(qi[0], 1)
        wi_r = to_sample_rows(wi[0], 0)
        sga_r = to_sample_rows(sga[0], 0)
        new_slots = lambda a_: _pad_rows(a_[0].reshape(dec_batch, dec_seq, a_.shape[-1]), LANE, 1)
        scores = _sample_scores(lidx, page_table, qi_r, wi_r, cache_ki2, n_phys, pages)
        thr, newsc = _sample_threshold(scores, qi_r, wi_r, new_slots(kibf), topk_s, dec_seq)
        a = _sample_attention(lidx, page_table, q_r, scores, thr, newsc, new_slots(kbf), new_slots(vbf), sga_r,
                              cache_k2, cache_v2, n_phys, pages)
        a = a[:, :dec_seq].reshape(1, n_dec, ATT_W)
        u_r = u[0].reshape(dec_batch, dec_seq, cw)
        cv = _conv(lidx, _pad_rows(u_r, SAMPLE_ROWS, 1), state_s, lambda l_, b: l_ * dec_batch + b, conv_w_p,
                   as_row(conv_b), as_row(conv_ln_g), as_row(conv_ln_b),
                   _pad_rows(sgc[0].reshape(dec_batch, dec_seq, cw), SAMPLE_ROWS, 1), SAMPLE_ROWS)
        cv = cv[:, :dec_seq].reshape(1, n_dec, cw)
        xs_new = _out(lidx, a, cv, gm, xs2, mod_s, mod_rows_s, w_ao, w_co, w_o, n_dec).reshape(xs.shape)
        past_state = lax.dynamic_index_in_dim(state_conv, l, 0, keepdims=False)
        new_conv_s = jnp.concatenate([past_state, u_r], axis=1)[:, -(CONV_K - 1):]
        outs_s = (k32.reshape(dec_batch, dec_seq, N_KV_HEADS, HEAD_DIM),
                  v32.reshape(dec_batch, dec_seq, N_KV_HEADS, HEAD_DIM),
                  ki32.reshape(dec_batch, dec_seq, IDX_DIM), new_conv_s)
        return (xp_new, xs_new), outs_p + outs_s

    (yp, ys), per_layer = lax.scan(layer, (x_prompt, x_sample), jnp.arange(n_layers, dtype=jnp.int32))
    return (yp, ys) + tuple(per_layer)
```

```python
import functools

import jax
import jax.numpy as jnp
from jax import lax
from jax.experimental import pallas as pl
from jax.experimental.pallas import tpu as pltpu

F32 = jnp.float32
BF16 = jnp.bfloat16
I32 = jnp.int32

N_HEADS = 8
HEAD_DIM = 128
N_KV_HEADS = 2
HEADS_PER_KV = N_HEADS // N_KV_HEADS
N_IDX_HEADS = 16
IDX_DIM = 64
TOPK_MAX = 256
CONV_K = 31
ROPE_THETA = 500000.0
EPS = 1e-6

LANE = 128
ATT_W = N_HEADS * HEAD_DIM
KV_W = N_KV_HEADS * HEAD_DIM
QI_W = N_IDX_HEADS * IDX_DIM
CONV_HALO = 32
SAMPLE_ROWS = 16
KEY_NEG_INF = -2139095041
KEY_POS_INF = 0x7F800000
VALUE_PASSES = 24
MAX_PASSES = VALUE_PASSES + 34
NEG_BIG = -1e30
LOG2E = 1.4426950408889634
VMEM_LIMIT = 56 * 1024 * 1024

_NT = (((1,), (1,)), ((), ()))


def _params(sem):
    return pltpu.CompilerParams(dimension_semantics=sem, vmem_limit_bytes=VMEM_LIMIT)


def _resident(block, imap):
    return pl.BlockSpec(block, imap, pipeline_mode=pl.Buffered(1))


def _silu(x):
    return x * jax.nn.sigmoid(x)


def _mod_kernel(c_ref, w_ref, b_ref, o_ref):
    a = _silu(c_ref[...]).astype(BF16)
    o_ref[0] = jnp.dot(a, w_ref[0].astype(BF16), preferred_element_type=F32) + b_ref[0]


def _modulation(c_all, w_ada, b_ada):
    n_layers, d, n3 = w_ada.shape
    rows = c_all.shape[0]
    tn = next(w for w in (512, 256, LANE, n3) if n3 % w == 0)
    return pl.pallas_call(
        _mod_kernel,
        grid=(n_layers, n3 // tn),
        in_specs=[pl.BlockSpec((rows, d), lambda l, j: (0, 0)),
                  pl.BlockSpec((1, d, tn), lambda l, j: (l, 0, j)),
                  pl.BlockSpec((1, 1, tn), lambda l, j: (l, 0, j))],
        out_specs=pl.BlockSpec((1, rows, tn), lambda l, j: (l, 0, j)),
        out_shape=jax.ShapeDtypeStruct((n_layers, rows, n3), F32),
        compiler_params=_params(("parallel", "parallel")),
        name="adaln_mod",
    )(c_all, w_ada, b_ada.reshape(n_layers, 1, n3))


def _norm_kernel(l_ref, x_ref, g_ref, sc_ref, sh_ref, o_ref):
    x = x_ref[0]
    y = x * lax.rsqrt(jnp.mean(x * x, axis=-1, keepdims=True) + EPS) * g_ref[0]
    o_ref[0] = (y * (1.0 + sc_ref[0]) + sh_ref[0]).astype(BF16)


def _norm(lidx, x, norm_g, mod, mod_rows, tm):
    nb, t, d = x.shape
    mod_imap = lambda part: (lambda b, i, l: (mod_rows(l[0], b), 0, part))
    return pl.pallas_call(
        _norm_kernel,
        grid_spec=pltpu.PrefetchScalarGridSpec(
            num_scalar_prefetch=1, grid=(nb, t // tm),
            in_specs=[pl.BlockSpec((1, tm, d), lambda b, i, l: (b, i, 0)),
                      pl.BlockSpec((1, 1, d), lambda b, i, l: (l[0], 0, 0)),
                      pl.BlockSpec((1, mod.shape[1], d), mod_imap(1)),
                      pl.BlockSpec((1, mod.shape[1], d), mod_imap(0))],
            out_specs=pl.BlockSpec((1, tm, d), lambda b, i, l: (b, i, 0))),
        out_shape=jax.ShapeDtypeStruct((nb, t, d), BF16),
        compiler_params=_params(("parallel", "parallel")),
        name="adaln_rmsnorm",
    )(lidx, x, norm_g, mod, mod)


def _rope(n, c, s1, s2, half):
    return n * c + pltpu.roll(n, LANE - half, 1) * s1 + pltpu.roll(n, half, 1) * s2


def _attn_in_kernel(l_ref, h_ref, w_ref, wkw_ref, qg_ref, kg_ref, gi_ref, rc_ref, rs1_ref, rs2_ref,
                    ic_ref, is1_ref, is2_ref,
                    q_ref, k32_ref, kbf_ref, v32_ref, vbf_ref, qi_ref, ki32_ref, kibf_ref, wi_ref, *, v_transposed):
    h = h_ref[0]
    rc, rs1, rs2 = rc_ref[...], rs1_ref[...], rs2_ref[...]
    ic, is1, is2 = ic_ref[...], is1_ref[...], is2_ref[...]

    def proj(c0, width):
        return jnp.dot(h, w_ref[0, :, c0:c0 + width], preferred_element_type=F32)

    def head_norm(z, g):
        return z * lax.rsqrt(jnp.mean(z * z, axis=-1, keepdims=True) + EPS) * g

    scale = HEAD_DIM ** -0.5 * LOG2E
    for pair in range(N_HEADS // 2):
        z2 = proj(pair * 2 * HEAD_DIM, 2 * HEAD_DIM)
        for j in range(2):
            z = z2[:, j * HEAD_DIM:(j + 1) * HEAD_DIM]
            r = _rope(head_norm(z, qg_ref[0]), rc, rs1, rs2, HEAD_DIM // 8)
            q_ref[0, 2 * pair + j] = (r * scale).astype(BF16)
    z2 = proj(ATT_W, KV_W)
    for g in range(N_KV_HEADS):
        z = z2[:, g * HEAD_DIM:(g + 1) * HEAD_DIM]
        r = _rope(head_norm(z, kg_ref[0]), rc, rs1, rs2, HEAD_DIM // 8)
        k32_ref[0, :, g * HEAD_DIM:(g + 1) * HEAD_DIM] = r
        kbf_ref[0, :, g * HEAD_DIM:(g + 1) * HEAD_DIM] = r.astype(BF16)
    z2 = proj(ATT_W + KV_W, KV_W)
    v32_ref[0] = z2
    if v_transposed:
        vbf_ref[0, 0] = z2.T.astype(BF16)
    else:
        vbf_ref[0] = z2.astype(BF16)

    lane = lax.broadcasted_iota(I32, (1, LANE), 1)
    lo = (lane < IDX_DIM).astype(F32)
    o_qi = ATT_W + 2 * KV_W
    for quad in range(N_IDX_HEADS // 4):
        z4 = proj(o_qi + quad * 2 * LANE, 2 * LANE)
        for j in range(2):
            r = _rope(z4[:, j * LANE:(j + 1) * LANE], ic, is1, is2, IDX_DIM // 8)
            qi_ref[0, 4 * quad + 2 * j] = (r * lo).astype(BF16)
            qi_ref[0, 4 * quad + 2 * j + 1] = (pltpu.roll(r, IDX_DIM, 1) * lo).astype(BF16)
    z = jnp.dot(h, wkw_ref[0], preferred_element_type=F32)
    ss = jnp.sum(z * z * lo, axis=-1, keepdims=True) * (1.0 / IDX_DIM)
    n = z * lax.rsqrt(ss + EPS) * gi_ref[0]
    r = _rope(n, ic, is1, is2, IDX_DIM // 8)
    ki32_ref[0] = r[:, :IDX_DIM]
    kibf_ref[0] = r.astype(BF16)
    wi_ref[0] = pltpu.roll(z, LANE - IDX_DIM, 1) * (N_IDX_HEADS ** -0.5 * IDX_DIM ** -0.5)


def _attn_in(lidx, h, w_a, w_kw, qg, kg, gi, rope_tabs, idx_tabs, tm, v_transposed):
    nb, t, d = h.shape
    na = w_a.shape[-1]
    row = lambda b, i, l: (b, i, 0)
    hm = lambda b, i, l: (b, 0, i, 0)
    lay = lambda b, i, l: (l[0], 0, 0)
    tab = pl.BlockSpec((tm, LANE), lambda b, i, l: (i, 0))
    if v_transposed:
        v_out = (jax.ShapeDtypeStruct((nb, t // tm, KV_W, tm), BF16),
                 pl.BlockSpec((1, 1, KV_W, tm), lambda b, i, l: (b, i, 0, 0)))
    else:
        v_out = (jax.ShapeDtypeStruct((nb, t, KV_W), BF16), pl.BlockSpec((1, tm, KV_W), row))
    outs = [
        (jax.ShapeDtypeStruct((nb, N_HEADS, t, HEAD_DIM), BF16), pl.BlockSpec((1, N_HEADS, tm, HEAD_DIM), hm)),
        (jax.ShapeDtypeStruct((nb, t, KV_W), F32), pl.BlockSpec((1, tm, KV_W), row)),
        (jax.ShapeDtypeStruct((nb, t, KV_W), BF16), pl.BlockSpec((1, tm, KV_W), row)),
        (jax.ShapeDtypeStruct((nb, t, KV_W), F32), pl.BlockSpec((1, tm, KV_W), row)),
        v_out,
        (jax.ShapeDtypeStruct((nb, N_IDX_HEADS, t, LANE), BF16), pl.BlockSpec((1, N_IDX_HEADS, tm, LANE), hm)),
        (jax.ShapeDtypeStruct((nb, t, IDX_DIM), F32), pl.BlockSpec((1, tm, IDX_DIM), row)),
        (jax.ShapeDtypeStruct((nb, t, LANE), BF16), pl.BlockSpec((1, tm, LANE), row)),
        (jax.ShapeDtypeStruct((nb, t, LANE), F32), pl.BlockSpec((1, tm, LANE), row)),
    ]
    return pl.pallas_call(
        functools.partial(_attn_in_kernel, v_transposed=v_transposed),
        grid_spec=pltpu.PrefetchScalarGridSpec(
            num_scalar_prefetch=1, grid=(nb, t // tm),
            in_specs=[pl.BlockSpec((1, tm, d), row),
                      _resident((1, d, na), lay), _resident((1, d, LANE), lay),
                      pl.BlockSpec((1, 1, LANE), lay), pl.BlockSpec((1, 1, LANE), lay),
                      pl.BlockSpec((1, 1, LANE), lay),
                      tab, tab, tab, tab, tab, tab],
            out_specs=[o[1] for o in outs]),
        out_shape=[o[0] for o in outs],
        compiler_params=_params(("parallel", "parallel")),
        name="attn_in_proj",
    )(lidx, h, w_a, w_kw, qg, kg, gi, *rope_tabs, *idx_tabs)


def _conv_in_kernel(l_ref, h_ref, w_ref, u_ref, sgc_ref, *, cw, bw):
    h = h_ref[0]
    for c0 in range(0, cw, bw):
        za = jnp.dot(h, w_ref[0, :, c0:c0 + bw], preferred_element_type=F32)
        zb = jnp.dot(h, w_ref[0, :, cw + c0:cw + c0 + bw], preferred_element_type=F32)
        u_ref[0, :, c0:c0 + bw] = za * jax.nn.sigmoid(zb)
        zg = jnp.dot(h, w_ref[0, :, 2 * cw + c0:2 * cw + c0 + bw], preferred_element_type=F32)
        sgc_ref[0, :, c0:c0 + bw] = _silu(zg).astype(BF16)


def _conv_in(lidx, h, w_b, tm):
    nb, t, d = h.shape
    cw = w_b.shape[-1] // 3
    row = lambda b, i, l: (b, i, 0)
    return pl.pallas_call(
        functools.partial(_conv_in_kernel, cw=cw, bw=min(256, cw)),
        grid_spec=pltpu.PrefetchScalarGridSpec(
            num_scalar_prefetch=1, grid=(nb, t // tm),
            in_specs=[pl.BlockSpec((1, tm, d), row),
                      _resident((1, d, 3 * cw), lambda b, i, l: (l[0], 0, 0))],
            out_specs=[pl.BlockSpec((1, tm, cw), row), pl.BlockSpec((1, tm, cw), row)]),
        out_shape=[jax.ShapeDtypeStruct((nb, t, cw), F32), jax.ShapeDtypeStruct((nb, t, cw), BF16)],
        compiler_params=_params(("parallel", "parallel")),
        name="conv_in_proj",
    )(lidx, h, w_b)


def _gate_in_kernel(l_ref, h_ref, wga_ref, wmg_ref, sga_ref, gm_ref, *, bw):
    h = h_ref[0]
    for c0 in range(0, ATT_W, bw):
        z = jnp.dot(h, wga_ref[0, :, c0:c0 + bw], preferred_element_type=F32)
        sga_ref[0, :, c0:c0 + bw] = _silu(z).astype(BF16)
    for c0 in range(0, gm_ref.shape[-1], bw):
        z = jnp.dot(h, wmg_ref[0, :, c0:c0 + bw], preferred_element_type=F32)
        gm_ref[0, :, c0:c0 + bw] = jax.nn.sigmoid(z).astype(BF16)


def _gate_in(lidx, h, w_ga, w_mg, tm):
    nb, t, d = h.shape
    nm = w_mg.shape[-1]
    row = lambda b, i, l: (b, i, 0)
    lay = lambda b, i, l: (l[0], 0, 0)
    return pl.pallas_call(
        functools.partial(_gate_in_kernel, bw=256),
        grid_spec=pltpu.PrefetchScalarGridSpec(
            num_scalar_prefetch=1, grid=(nb, t // tm),
            in_specs=[pl.BlockSpec((1, tm, d), row), _resident((1, d, ATT_W), lay), _resident((1, d, nm), lay)],
            out_specs=[pl.BlockSpec((1, tm, ATT_W), row), pl.BlockSpec((1, tm, nm), row)]),
        out_shape=[jax.ShapeDtypeStruct((nb, t, ATT_W), BF16), jax.ShapeDtypeStruct((nb, t, nm), BF16)],
        compiler_params=_params(("parallel", "parallel")),
        name="gate_in_proj",
    )(lidx, h, w_ga, w_mg)


def _key_to_float(key):
    b = key ^ ((key >> 31) & jnp.int32(0x7FFFFFFF))
    return lax.bitcast_convert_type(b, F32)


def _float_to_key(x):
    b = lax.bitcast_convert_type(x, I32)
    return b ^ ((b >> 31) & jnp.int32(0x7FFFFFFF))


def _kth_largest(count_ge, lo_f, hi_f, n_ge_lo, k):
    kf = jnp.float32(k)
    select_all = n_ge_lo <= kf
    lo0 = jnp.where(select_all, jnp.int32(KEY_NEG_INF + 1), _float_to_key(lo_f))
    hi0 = _float_to_key(hi_f) + 1

    def one_pass(it, lo, hi, clo, active):
        lo_v = _key_to_float(lo)
        hi_v = _key_to_float(jnp.minimum(hi, jnp.int32(KEY_POS_INF)))
        mid_value = _float_to_key(0.5 * lo_v + 0.5 * hi_v)
        mid_image = (lo & hi) + ((lo ^ hi) >> 1)
        cand = jnp.where(it < VALUE_PASSES, mid_value, mid_image)
        cand = jnp.minimum(jnp.maximum(cand, lo + 1), hi - 1)
        cnt = count_ge(_key_to_float(cand))
        up = (cnt >= kf) & (active > 0.0)
        down = (cnt < kf) & (active > 0.0)
        lo = jnp.where(up, cand, lo)
        clo = jnp.where(up, cnt, clo)
        hi = jnp.where(down, cand, hi)
        active = jnp.where((clo == kf) | (hi <= lo + 1), 0.0, active)
        return lo, hi, clo, active

    def cond(st):
        return (st[0] < MAX_PASSES) & (jnp.sum(st[4]) > 0.0)

    def body(st):
        it, lo, hi, clo, active = st
        lo, hi, clo, active = one_pass(it, lo, hi, clo, active)
        lo, hi, clo, active = one_pass(it + 1, lo, hi, clo, active)
        return it + 2, lo, hi, clo, active

    active0 = jnp.where(select_all, 0.0, 1.0)
    st = lax.while_loop(cond, body, (jnp.int32(0), lo0, hi0, n_ge_lo, active0))
    return _key_to_float(jnp.maximum(st[1], jnp.int32(KEY_NEG_INF + 1)))


def _fold_rows(x, op):
    r, n = x.shape
    if r > 64:
        x = op(x.reshape(r // 64, 64, n), axis=0)
    return op(x.reshape(x.shape[0] // 8, 8, n), axis=0)


def _index_scores(qi, wi, kc, rows):
    sc = jnp.zeros((rows, kc.shape[0]), F32)
    for hg in range(N_IDX_HEADS // 4):
        d = lax.dot_general(qi[hg * 4 * rows:(hg + 1) * 4 * rows], kc, _NT, preferred_element_type=F32)
        for j in range(4):
            hd = hg * 4 + j
            sc = sc + jnp.maximum(d[j * rows:(j + 1) * rows], 0.0) * wi[:, hd:hd + 1]
    return sc


def _softmax_step(s, vc, mask, m_ref, l_ref, acc_ref, rows):
    ck = vc.shape[0]
    s = jnp.where(mask[None], s.reshape(HEADS_PER_KV, rows, ck), NEG_BIG).reshape(HEADS_PER_KV * rows, ck)
    m_prev = m_ref[:, :1]
    m_new = jnp.maximum(m_prev, jnp.max(s, axis=-1, keepdims=True))
    alpha = jnp.exp2(m_prev - m_new)
    p = jnp.exp2(s - m_new)
    l_ref[...] = jnp.broadcast_to(alpha * l_ref[:, :1] + jnp.sum(p, axis=-1, keepdims=True), l_ref.shape)
    acc_ref[...] = alpha * acc_ref[...] + jnp.dot(p.astype(BF16), vc, preferred_element_type=F32)
    m_ref[...] = jnp.broadcast_to(m_new, m_ref.shape)


def _softmax_init(m_ref, l_ref, acc_ref):
    m_ref[...] = jnp.full(m_ref.shape, NEG_BIG, F32)
    l_ref[...] = jnp.zeros(l_ref.shape, F32)
    acc_ref[...] = jnp.zeros(acc_ref.shape, F32)


def _softmax_finish(g, sga_ref, o_ref, l_ref, acc_ref, rows):
    o = acc_ref[...] / l_ref[:, :1]
    for j in range(HEADS_PER_KV):
        c0 = (g * HEADS_PER_KV + j) * HEAD_DIM
        gate = sga_ref[0, :, c0:c0 + HEAD_DIM].astype(F32)
        o_ref[0, :, c0:c0 + HEAD_DIM] = (o[j * rows:(j + 1) * rows] * gate).astype(BF16)


def _prompt_attn_kernel(q_ref, qi_ref, wi_ref, k_ref, vt_ref, ki_ref, sga_ref, o_ref,
                        sc_s, m_s, l_s, acc_s, *, tq, ck, topk):
    row0 = pl.program_id(1) * tq
    nch = (row0 + tq + ck - 1) // ck
    wit = wi_ref[0].T
    kpos = lax.broadcasted_iota(I32, (ck, tq), 0)
    qpos = row0 + lax.broadcasted_iota(I32, (ck, tq), 1)

    def score_chunk(c, carry):
        kc = ki_ref[0, pl.ds(pl.multiple_of(c * ck, ck), ck), :]
        sc = jnp.zeros((ck, tq), F32)
        for hd in range(N_IDX_HEADS):
            d = lax.dot_general(kc, qi_ref[0, hd], _NT, preferred_element_type=F32)
            sc = sc + jnp.maximum(d, 0.0) * wit[hd:hd + 1, :]
        sc_s[c] = jnp.where(kpos + c * ck <= qpos, sc, -jnp.inf)
        return carry

    lax.fori_loop(0, nch, score_chunk, 0)

    def count_ge(thr):
        def body(c, acc):
            return acc + _fold_rows(jnp.where(sc_s[c] >= thr, 1.0, 0.0), jnp.sum)

        acc = lax.fori_loop(0, nch, body, jnp.zeros((8, tq), F32))
        return jnp.sum(acc, axis=0, keepdims=True)

    def fold_minmax(s, carry, masked):
        mx, mn = carry
        mx = jnp.maximum(mx, _fold_rows(s, jnp.max))
        if masked:
            s = jnp.where(s == -jnp.inf, jnp.inf, s)
        return mx, jnp.minimum(mn, _fold_rows(s, jnp.min))

    init = (jnp.full((8, tq), -jnp.inf, F32), jnp.full((8, tq), jnp.inf, F32))
    mm = lax.fori_loop(0, nch - 1, lambda c, carry: fold_minmax(sc_s[c], carry, False), init)
    mx, mn = fold_minmax(sc_s[nch - 1], mm, True)
    n_visible = (qpos[:1] + 1).astype(F32)
    thr = _kth_largest(count_ge, jnp.min(mn, axis=0, keepdims=True), jnp.max(mx, axis=0, keepdims=True),
                       n_visible, topk)

    _softmax_init(m_s, l_s, acc_s)

    def attend_chunk(c, carry):
        bias = jnp.where(sc_s[c] >= thr, 0.0, NEG_BIG)
        off = pl.multiple_of(c * ck, ck)

        def qk(g):
            kc = k_ref[0, pl.ds(off, ck), g * HEAD_DIM:(g + 1) * HEAD_DIM]
            qg = q_ref[0, g * HEADS_PER_KV:(g + 1) * HEADS_PER_KV].reshape(HEADS_PER_KV * tq, HEAD_DIM)
            return lax.dot_general(kc, qg, _NT, preferred_element_type=F32)

        logits = [qk(g) for g in range(N_KV_HEADS)]
        for g in range(N_KV_HEADS):
            ps, alphas = [], []
            for j in range(HEADS_PER_KV):
                hd = g * HEADS_PER_KV + j
                s = logits[g][:, j * tq:(j + 1) * tq] + bias
                m_prev = m_s[hd]
                m_new = jnp.maximum(m_prev, jnp.max(_fold_rows(s, jnp.max), axis=0, keepdims=True))
                alpha = jnp.exp2(m_prev - m_new)
                p = jnp.exp2(s - m_new)
                l_s[hd] = alpha * l_s[hd] + jnp.sum(_fold_rows(p, jnp.sum), axis=0, keepdims=True)
                m_s[hd] = m_new
                ps.append(p.astype(BF16))
                alphas.append(alpha)
            vtc = vt_ref[0, c, g * HEAD_DIM:(g + 1) * HEAD_DIM, :]
            pv = jnp.dot(vtc, jnp.concatenate(ps, axis=1), preferred_element_type=F32)
            acc_s[g] = jnp.concatenate(alphas, axis=1) * acc_s[g] + pv
        return carry

    lax.fori_loop(0, nch, attend_chunk, 0)
    for hd in range(N_HEADS):
        g, j = divmod(hd, HEADS_PER_KV)
        o = (acc_s[g, :, j * tq:(j + 1) * tq] / l_s[hd]).T
        gate = sga_ref[0, :, hd * HEAD_DIM:(hd + 1) * HEAD_DIM].astype(F32)
        o_ref[0, :, hd * HEAD_DIM:(hd + 1) * HEAD_DIM] = (o * gate).astype(BF16)


def _prompt_attention(q, qi, wi, kbf, vt, kibf, sga, topk, tq, ck):
    nb, _, t, _ = q.shape
    assert vt.shape == (nb, t // ck, KV_W, ck)
    full = lambda b, i: (b, 0, 0)
    row = lambda b, i: (b, i, 0)
    hm = lambda b, i: (b, 0, i, 0)
    return pl.pallas_call(
        functools.partial(_prompt_attn_kernel, tq=tq, ck=ck, topk=topk),
        grid=(nb, t // tq),
        in_specs=[pl.BlockSpec((1, N_HEADS, tq, HEAD_DIM), hm),
                  pl.BlockSpec((1, N_IDX_HEADS, tq, LANE), hm),
                  pl.BlockSpec((1, tq, LANE), row),
                  pl.BlockSpec((1, t, KV_W), full),
                  pl.BlockSpec((1, t // ck, KV_W, ck), lambda b, i: (b, 0, 0, 0)),
                  pl.BlockSpec((1, t, LANE), full),
                  pl.BlockSpec((1, tq, ATT_W), row)],
        out_specs=pl.BlockSpec((1, tq, ATT_W), row),
        out_shape=jax.ShapeDtypeStruct((nb, t, ATT_W), BF16),
        scratch_shapes=[pltpu.VMEM((t // ck, ck, tq), F32),
                        pltpu.VMEM((N_HEADS, 1, tq), F32),
                        pltpu.VMEM((N_HEADS, 1, tq), F32),
                        pltpu.VMEM((N_KV_HEADS, HEAD_DIM, HEADS_PER_KV * tq), F32)],
        compiler_params=_params(("parallel", "arbitrary")),
        name="prompt_sparse_attention",
    )(q, qi, wi, kbf, vt, kibf, sga)


def _paged_fetch(l_ref, pt_ref, hbm_refs, raw_refs, sems, *, pages, n_phys):
    b, j = pl.program_id(0), pl.program_id(1)
    nj = pl.num_programs(1)
    step = b * nj + j
    slot = step % 2

    def copies(bb, jj, sl):
        out = []
        for p in range(pages):
            row = l_ref[0] * n_phys + pt_ref[bb, jj * pages + p]
            for a, (hbm, raw) in enumerate(zip(hbm_refs, raw_refs)):
                out.append(pltpu.make_async_copy(hbm.at[row], raw.at[sl, p], sems.at[a, sl]))
        return out

    @pl.when(step == 0)
    def _():
        for cp in copies(b, j, slot):
            cp.start()

    @pl.when(step + 1 < pl.num_programs(0) * nj)
    def _():
        for cp in copies((step + 1) // nj, (step + 1) % nj, 1 - slot):
            cp.start()

    for cp in copies(b, j, slot):
        cp.wait()
    return slot


def _sample_scores_kernel(l_ref, pt_ref, qi_ref, wi_ref, cache_hbm, o_ref, raw, sems, kbuf, *, pages, n_phys):
    slot = _paged_fetch(l_ref, pt_ref, [cache_hbm], [raw], sems, pages=pages, n_phys=n_phys)
    page = raw.shape[2]
    for p in range(pages):
        kbuf[p * page:(p + 1) * page, :] = raw[slot, p].astype(BF16)
    qi = qi_ref[0].reshape(N_IDX_HEADS * SAMPLE_ROWS, LANE)[:, :IDX_DIM]
    o_ref[0] = _index_scores(qi, wi_ref[0], kbuf[...], SAMPLE_ROWS)


def _sample_scores(lidx, page_table, qi, wi, cache_kidx, n_phys, pages):
    nb, n_pages = page_table.shape
    page = cache_kidx.shape[1]
    return pl.pallas_call(
        functools.partial(_sample_scores_kernel, pages=pages, n_phys=n_phys),
        grid_spec=pltpu.PrefetchScalarGridSpec(
            num_scalar_prefetch=2, grid=(nb, n_pages // pages),
            in_specs=[pl.BlockSpec((1, N_IDX_HEADS, SAMPLE_ROWS, LANE), lambda b, j, l, pt: (b, 0, 0, 0)),
                      pl.BlockSpec((1, SAMPLE_ROWS, LANE), lambda b, j, l, pt: (b, 0, 0)),
                      pl.BlockSpec(memory_space=pl.ANY)],
            out_specs=pl.BlockSpec((1, SAMPLE_ROWS, pages * page), lambda b, j, l, pt: (b, 0, j)),
            scratch_shapes=[pltpu.VMEM((2, pages, page, IDX_DIM), F32),
                            pltpu.SemaphoreType.DMA((1, 2)),
                            pltpu.VMEM((pages * page, IDX_DIM), BF16)]),
        out_shape=jax.ShapeDtypeStruct((nb, SAMPLE_ROWS, n_pages * page), F32),
        compiler_params=_params(("arbitrary", "arbitrary")),
        name="sample_index_scores",
    )(lidx, page_table, qi, wi, cache_kidx)


def _sample_threshold_kernel(sc_ref, qi_ref, wi_ref, kin_ref, thr_ref, newsc_ref, *, topk, n_new):
    rows = SAMPLE_ROWS
    qi = qi_ref[0].reshape(N_IDX_HEADS * rows, LANE)
    sc_new = _index_scores(qi, wi_ref[0], kin_ref[0], rows)
    r = lax.broadcasted_iota(I32, sc_new.shape, 0)
    c = lax.broadcasted_iota(I32, sc_new.shape, 1)
    sc_new = jnp.where((c <= r) & (c < n_new), sc_new, -jnp.inf)
    newsc_ref[0] = sc_new

    def count_ge(thr):
        tb = jnp.broadcast_to(thr, (rows, LANE))
        accs = [jnp.where(sc_new >= tb, 1.0, 0.0)] + [jnp.zeros((rows, LANE), F32)] * 7
        for j in range(sc_ref.shape[-1] // LANE):
            accs[j % 8] = accs[j % 8] + jnp.where(sc_ref[0, :, j * LANE:(j + 1) * LANE] >= tb, 1.0, 0.0)
        acc = (accs[0] + accs[1]) + (accs[2] + accs[3]) + ((accs[4] + accs[5]) + (accs[6] + accs[7]))
        return jnp.sum(acc, axis=-1, keepdims=True)

    past = sc_ref[0]
    mx = jnp.maximum(jnp.max(past, axis=-1, keepdims=True), jnp.max(sc_new, axis=-1, keepdims=True))
    mn = jnp.minimum(jnp.min(past, axis=-1, keepdims=True),
                     jnp.min(jnp.where(sc_new == -jnp.inf, jnp.inf, sc_new), axis=-1, keepdims=True))
    n_visible = (past.shape[-1] + jnp.minimum(r[:, :1] + 1, n_new)).astype(F32)
    thr_ref[0] = jnp.broadcast_to(_kth_largest(count_ge, mn, mx, n_visible, topk), (rows, LANE))


def _sample_threshold(scores, qi, wi, ki_new, topk, n_new):
    nb, rows, n_keys = scores.shape
    b3 = lambda b: (b, 0, 0)
    return pl.pallas_call(
        functools.partial(_sample_threshold_kernel, topk=topk, n_new=n_new),
        grid=(nb,),
        in_specs=[pl.BlockSpec((1, rows, n_keys), b3),
                  pl.BlockSpec((1, N_IDX_HEADS, rows, LANE), lambda b: (b, 0, 0, 0)),
                  pl.BlockSpec((1, rows, LANE), b3),
                  pl.BlockSpec((1, LANE, LANE), b3)],
        out_specs=[pl.BlockSpec((1, rows, LANE), b3), pl.BlockSpec((1, rows, LANE), b3)],
        out_shape=[jax.ShapeDtypeStruct((nb, rows, LANE), F32), jax.ShapeDtypeStruct((nb, rows, LANE), F32)],
        compiler_params=_params(("parallel",)),
        name="sample_topk_threshold",
    )(scores, qi, wi, ki_new)


def _sample_attn_kernel(l_ref, pt_ref, q_ref, sc_ref, thr_ref, newsc_ref, knew_ref, vnew_ref, sga_ref,
                        ck_hbm, cv_hbm, o_ref, kraw, vraw, sems, kbuf, vbuf, m_s, l_s, acc_s, *, pages, n_phys):
    slot = _paged_fetch(l_ref, pt_ref, [ck_hbm, cv_hbm], [kraw, vraw], sems, pages=pages, n_phys=n_phys)
    rows = SAMPLE_ROWS
    page = kraw.shape[2] // N_KV_HEADS
    j = pl.program_id(1)
    thr = thr_ref[0][:, :1]

    @pl.when(j == 0)
    def _():
        for g in range(N_KV_HEADS):
            _softmax_init(m_s.at[g], l_s.at[g], acc_s.at[g])

    def q_group(g):
        return q_ref[0, g * HEADS_PER_KV:(g + 1) * HEADS_PER_KV].reshape(HEADS_PER_KV * rows, HEAD_DIM)

    for g in range(N_KV_HEADS):
        for p in range(pages):
            kbuf[g, p * page:(p + 1) * page, :] = kraw[slot, p, pl.ds(g, page, stride=N_KV_HEADS), :].astype(BF16)
            vbuf[g, p * page:(p + 1) * page, :] = vraw[slot, p, pl.ds(g, page, stride=N_KV_HEADS), :].astype(BF16)
    mask = sc_ref[0] >= thr
    logits = [lax.dot_general(q_group(g), kbuf[g], _NT, preferred_element_type=F32) for g in range(N_KV_HEADS)]
    for g in range(N_KV_HEADS):
        _softmax_step(logits[g], vbuf[g], mask, m_s.at[g], l_s.at[g], acc_s.at[g], rows)

    @pl.when(j == pl.num_programs(1) - 1)
    def _():
        mask_new = newsc_ref[0] >= thr
        for g in range(N_KV_HEADS):
            kc = knew_ref[0, :, g * HEAD_DIM:(g + 1) * HEAD_DIM]
            vc = vnew_ref[0, :, g * HEAD_DIM:(g + 1) * HEAD_DIM]
            s = lax.dot_general(q_group(g), kc, _NT, preferred_element_type=F32)
            _softmax_step(s, vc, mask_new, m_s.at[g], l_s.at[g], acc_s.at[g], rows)
            _softmax_finish(g, sga_ref, o_ref, l_s.at[g], acc_s.at[g], rows)


def _sample_attention(lidx, page_table, q, scores, thr, newsc, k_new, v_new, sga, cache_k, cache_v, n_phys, pages):
    nb, n_pages = page_table.shape
    page = cache_k.shape[1] // N_KV_HEADS
    rows = SAMPLE_ROWS
    b3 = lambda b, j, l, pt: (b, 0, 0)
    return pl.pallas_call(
        functools.partial(_sample_attn_kernel, pages=pages, n_phys=n_phys),
        grid_spec=pltpu.PrefetchScalarGridSpec(
            num_scalar_prefetch=2, grid=(nb, n_pages // pages),
            in_specs=[pl.BlockSpec((1, N_HEADS, rows, HEAD_DIM), lambda b, j, l, pt: (b, 0, 0, 0)),
                      pl.BlockSpec((1, rows, pages * page), lambda b, j, l, pt: (b, 0, j)),
                      pl.BlockSpec((1, rows, LANE), b3), pl.BlockSpec((1, rows, LANE), b3),
                      pl.BlockSpec((1, LANE, KV_W), b3), pl.BlockSpec((1, LANE, KV_W), b3),
                      pl.BlockSpec((1, rows, ATT_W), b3),
                      pl.BlockSpec(memory_space=pl.ANY), pl.BlockSpec(memory_space=pl.ANY)],
            out_specs=pl.BlockSpec((1, rows, ATT_W), b3),
            scratch_shapes=[pltpu.VMEM((2, pages, page * N_KV_HEADS, HEAD_DIM), F32),
                            pltpu.VMEM((2, pages, page * N_KV_HEADS, HEAD_DIM), F32),
                            pltpu.SemaphoreType.DMA((2, 2)),
                            pltpu.VMEM((N_KV_HEADS, pages * page, HEAD_DIM), BF16),
                            pltpu.VMEM((N_KV_HEADS, pages * page, HEAD_DIM), BF16),
                            pltpu.VMEM((N_KV_HEADS, HEADS_PER_KV * rows, LANE), F32),
                            pltpu.VMEM((N_KV_HEADS, HEADS_PER_KV * rows, LANE), F32),
                            pltpu.VMEM((N_KV_HEADS, HEADS_PER_KV * rows, HEAD_DIM), F32)]),
        out_shape=jax.ShapeDtypeStruct((nb, rows, ATT_W), BF16),
        compiler_params=_params(("arbitrary", "arbitrary")),
        name="sample_sparse_attention",
    )(lidx, page_table, q, scores, thr, newsc, k_new, v_new, sga, cache_k, cache_v)


def _conv_kernel(l_ref, u_ref, st_ref, w_ref, b_ref, g_ref, beta_ref, sgc_ref, o_ref, ext_s, sh_s, d_s,
                 *, tt, rb, cb):
    cw = u_ref.shape[-1]

    @pl.when(pl.program_id(1) == 0)
    def _():
        ext_s[0:CONV_HALO] = st_ref[0]

    ext_s[CONV_HALO:CONV_HALO + tt] = u_ref[0]
    first = CONV_HALO - (CONV_K - 1)
    n_sh = sh_s.shape[1]
    for ph in range(1, 8):
        sh_s[ph - 1] = ext_s[ph:ph + n_sh, :]
    for c0 in range(0, cw, cb):
        for r0 in range(0, tt, rb):
            acc = jnp.zeros((rb, cb), F32)
            for j in range(CONV_K):
                ph, base = (first + j) % 8, (first + j) // 8 * 8 + r0
                if ph == 0:
                    x = ext_s[base:base + rb, c0:c0 + cb]
                else:
                    x = sh_s[ph - 1, base:base + rb, c0:c0 + cb]
                acc = acc + x * w_ref[0, j:j + 1, c0:c0 + cb]
            d_s[r0:r0 + rb, c0:c0 + cb] = acc
    d = d_s[...] + b_ref[0]
    mu = jnp.mean(d, axis=-1, keepdims=True)
    xc = d - mu
    y = xc * lax.rsqrt(jnp.mean(xc * xc, axis=-1, keepdims=True) + EPS) * g_ref[0] + beta_ref[0]
    o_ref[0] = (_silu(y) * sgc_ref[0].astype(F32)).astype(BF16)
    if tt >= CONV_HALO:
        ext_s[0:CONV_HALO] = ext_s[tt:tt + CONV_HALO]


def _conv(lidx, u, state, state_rows, conv_w, conv_b, ln_g, ln_b, sgc, tt):
    nb, t, cw = u.shape
    row = lambda b, i, l: (b, i, 0)
    lay = lambda b, i, l: (l[0], 0, 0)
    rb = min(32, tt)
    return pl.pallas_call(
        functools.partial(_conv_kernel, tt=tt, rb=rb, cb=min(256, cw)),
        grid_spec=pltpu.PrefetchScalarGridSpec(
            num_scalar_prefetch=1, grid=(nb, t // tt),
            in_specs=[pl.BlockSpec((1, tt, cw), row),
                      pl.BlockSpec((1, CONV_HALO, cw), lambda b, i, l: (state_rows(l[0], b), 0, 0)),
                      pl.BlockSpec((1, CONV_HALO, cw), lay),
                      pl.BlockSpec((1, 1, cw), lay), pl.BlockSpec((1, 1, cw), lay), pl.BlockSpec((1, 1, cw), lay),
                      pl.BlockSpec((1, tt, cw), row)],
            out_specs=pl.BlockSpec((1, tt, cw), row),
            scratch_shapes=[pltpu.VMEM((CONV_HALO + tt, cw), F32),
                            pltpu.VMEM((7, tt + CONV_HALO - 8, cw), F32),
                            pltpu.VMEM((tt, cw), F32)]),
        out_shape=jax.ShapeDtypeStruct((nb, t, cw), BF16),
        compiler_params=_params(("parallel", "arbitrary")),
        name="conformer_conv",
    )(lidx, u, state, conv_w, conv_b, ln_g, ln_b, sgc)


def _out_kernel(l_ref, a_ref, cv_ref, gm_ref, x_ref, gate_ref, wa_ref, wc_ref, wo_ref, o_ref):
    d = x_ref.shape[-1]
    y_att = jnp.dot(a_ref[0], wa_ref[0], preferred_element_type=F32)
    y_conv = jnp.dot(cv_ref[0], wc_ref[0], preferred_element_type=F32)
    merged = gm_ref[0, :, :d].astype(F32) * y_att + gm_ref[0, :, d:].astype(F32) * y_conv
    out = jnp.dot(merged.astype(BF16), wo_ref[0], preferred_element_type=F32)
    o_ref[0] = x_ref[0] + gate_ref[0] * out


def _out(lidx, a, cv, gm, x, mod, mod_rows, w_ao, w_co, w_o, tm):
    nb, t, d = x.shape
    cw = cv.shape[-1]
    row = lambda b, i, l: (b, i, 0)
    lay = lambda b, i, l: (l[0], 0, 0)
    return pl.pallas_call(
        _out_kernel,
        grid_spec=pltpu.PrefetchScalarGridSpec(
            num_scalar_prefetch=1, grid=(nb, t // tm),
            in_specs=[pl.BlockSpec((1, tm, ATT_W), row), pl.BlockSpec((1, tm, cw), row),
                      pl.BlockSpec((1, tm, 2 * d), row), pl.BlockSpec((1, tm, d), row),
                      pl.BlockSpec((1, mod.shape[1], d), lambda b, i, l: (mod_rows(l[0], b), 0, 2)),
                      _resident((1, ATT_W, d), lay), _resident((1, cw, d), lay), _resident((1, d, d), lay)],
            out_specs=pl.BlockSpec((1, tm, d), row)),
        out_shape=jax.ShapeDtypeStruct((nb, t, d), F32),
        input_output_aliases={4: 0},
        compiler_params=_params(("parallel", "parallel")),
        name="merge_out_proj",
    )(lidx, a, cv, gm, x, mod, w_ao, w_co, w_o)


def _rope_tables(pos, head_dim):
    rd = head_dim // 4
    half = rd // 2
    inv = ROPE_THETA ** (-jnp.arange(half, dtype=F32) / half)
    ang = pos.astype(F32)[:, None] * inv[None, :]
    cos, sin = jnp.cos(ang), jnp.sin(ang)
    n = pos.shape[0]
    pad = lambda w: jnp.zeros((n, w), F32)
    c = jnp.concatenate([cos, cos, jnp.ones((n, head_dim - rd), F32)], axis=-1)
    s1 = jnp.concatenate([-sin, pad(head_dim - half)], axis=-1)
    s2 = jnp.concatenate([pad(half), sin, pad(head_dim - rd)], axis=-1)
    rep = LANE // head_dim
    return tuple(jnp.tile(a, (1, rep)) for a in (c, s1, s2))


def _pad_rows(a, rows, axis):
    pad = [(0, 0)] * a.ndim
    pad[axis] = (0, rows - a.shape[axis])
    return jnp.pad(a, pad)


def kernel(x_prompt, x_sample, c_prompt, c_sample, cache_k, cache_v, cache_kidx, state_conv, page_table, norm_g, w_ada, b_ada, w_in, q_norm_g, k_norm_g, idx_k_norm_g, conv_w, conv_b, conv_ln_g, conv_ln_b, w_attn_out, w_conv_out, w_out):
    batch, seq, d = x_prompt.shape
    dec_batch, dec_seq, _ = x_sample.shape
    n_layers, n_phys, page = cache_k.shape[:3]
    n_pages = page_table.shape[1]
    past_len = n_pages * page
    cw = conv_w.shape[-1]
    topk_p = min(TOPK_MAX, seq // 4)
    topk_s = min(TOPK_MAX, (past_len + dec_seq) // 4)
    n_dec = dec_batch * dec_seq
    assert dec_seq <= SAMPLE_ROWS and CONV_K - 1 <= CONV_HALO

    o_ki = ATT_W + 2 * KV_W + QI_W
    o_ga = o_ki + IDX_DIM + N_IDX_HEADS
    o_glu = o_ga + ATT_W
    o_mg = o_glu + 3 * cw
    w_in16 = w_in.astype(BF16)
    w_a = w_in16[..., :o_ki]
    w_kw = _pad_rows(w_in16[..., o_ki:o_ga], LANE, 2)
    w_ga = w_in16[..., o_ga:o_glu]
    w_b = w_in16[..., o_glu:o_mg]
    w_mg = w_in16[..., o_mg:]
    w_ao, w_co, w_o = w_attn_out.astype(BF16), w_conv_out.astype(BF16), w_out.astype(BF16)
    qg = q_norm_g.reshape(n_layers, 1, HEAD_DIM)
    kg = k_norm_g.reshape(n_layers, 1, HEAD_DIM)
    gi = _pad_rows(idx_k_norm_g, LANE, 1).reshape(n_layers, 1, LANE)
    conv_w_p = _pad_rows(conv_w, CONV_HALO, 1)
    as_row = lambda a: a.reshape(n_layers, 1, a.shape[-1])

    n_c = batch + dec_batch
    c_all = _pad_rows(jnp.concatenate([c_prompt, c_sample], axis=0), -(-n_c // 8) * 8, 0)
    mod = _modulation(c_all, w_ada, b_ada)
    mod_p = mod[:, :batch].reshape(n_layers * batch, 1, 3 * d)
    mod_s = jnp.repeat(mod[:, batch:n_c], dec_seq, axis=1)
    mod_rows_p = lambda l, b: l * batch + b
    mod_rows_s = lambda l, b: l

    rope_p = _rope_tables(jnp.arange(seq, dtype=jnp.int32), HEAD_DIM)
    idx_p = _rope_tables(jnp.arange(seq, dtype=jnp.int32), IDX_DIM)
    pos_s = jnp.tile(past_len + jnp.arange(dec_seq, dtype=jnp.int32), dec_batch)
    rope_s = _rope_tables(pos_s, HEAD_DIM)
    idx_s = _rope_tables(pos_s, IDX_DIM)

    cache_k2 = cache_k.reshape(n_layers * n_phys, page * N_KV_HEADS, HEAD_DIM)
    cache_v2 = cache_v.reshape(n_layers * n_phys, page * N_KV_HEADS, HEAD_DIM)
    cache_ki2 = cache_kidx.reshape(n_layers * n_phys, page, IDX_DIM)
    state_p = jnp.zeros((1, CONV_HALO, cw), F32)
    state_s = jnp.pad(state_conv, ((0, 0), (0, 0), (CONV_HALO - (CONV_K - 1), 0), (0, 0)))
    state_s = state_s.reshape(n_layers * dec_batch, CONV_HALO, cw)

    tm = min(512, seq)
    tq = min(256, seq)
    ck = tm
    tt = min(256, seq)
    pages = min(16, n_pages)

    def to_sample_rows(a, axis):
        shp = a.shape[:axis] + (dec_batch, dec_seq) + a.shape[axis + 1:]
        a = jnp.moveaxis(a.reshape(shp), axis, 0)
        return _pad_rows(a, SAMPLE_ROWS, axis + 1)

    def layer(carry, l):
        xp, xs = carry
        lidx = jnp.reshape(l, (1,)).astype(jnp.int32)

        h = _norm(lidx, xp, as_row(norm_g), mod_p, mod_rows_p, tm)
        q, k32, kbf, v32, vt, qi, ki32, kibf, wi = _attn_in(lidx, h, w_a, w_kw, qg, kg, gi, rope_p, idx_p, tm, True)
        u, sgc = _conv_in(lidx, h, w_b, tm)
        sga, gm = _gate_in(lidx, h, w_ga, w_mg, tm)
        a = _prompt_attention(q, qi, wi, kbf, vt, kibf, sga, topk_p, tq, ck)
        cv = _conv(lidx, u, state_p, lambda l_, b: 0, conv_w_p, as_row(conv_b), as_row(conv_ln_g),
                   as_row(conv_ln_b), sgc, tt)
        xp_new = _out(lidx, a, cv, gm, xp, mod_p, mod_rows_p, w_ao, w_co, w_o, tm)
        outs_p = (k32.reshape(batch, seq, N_KV_HEADS, HEAD_DIM), v32.reshape(batch, seq, N_KV_HEADS, HEAD_DIM),
                  ki32, u[:, seq - (CONV_K - 1):])

        xs2 = xs.reshape(1, n_dec, d)
        h = _norm(lidx, xs2, as_row(norm_g), mod_s, mod_rows_s, n_dec)
        q, k32, kbf, v32, vbf, qi, ki32, kibf, wi = _attn_in(lidx, h, w_a, w_kw, qg, kg, gi, rope_s, idx_s, n_dec,
                                                             False)
        u, sgc = _conv_in(lidx, h, w_b, n_dec)
        sga, gm = _gate_in(lidx, h, w_ga, w_mg, n_dec)
        q_r = to_sample_rows(q[0], 1)
        qi_r = to_sample_rows(qi[0], 1)
        wi_r = to_sample_rows(wi[0], 0)
        sga_r = to_sample_rows(sga[0], 0)
        new_slots = lambda a_: _pad_rows(a_[0].reshape(dec_batch, dec_seq, a_.shape[-1]), LANE, 1)
        scores = _sample_scores(lidx, page_table, qi_r, wi_r, cache_ki2, n_phys, pages)
        thr, newsc = _sample_threshold(scores, qi_r, wi_r, new_slots(kibf), topk_s, dec_seq)
        a = _sample_attention(lidx, page_table, q_r, scores, thr, newsc, new_slots(kbf), new_slots(vbf), sga_r,
                              cache_k2, cache_v2, n_phys, pages)
        a = a[:, :dec_seq].reshape(1, n_dec, ATT_W)
        u_r = u[0].reshape(dec_batch, dec_seq, cw)
        cv = _conv(lidx, _pad_rows(u_r, SAMPLE_ROWS, 1), state_s, lambda l_, b: l_ * dec_batch + b, conv_w_p,
                   as_row(conv_b), as_row(conv_ln_g), as_row(conv_ln_b),
                   _pad_rows(sgc[0].reshape(dec_batch, dec_seq, cw), SAMPLE_ROWS, 1), SAMPLE_ROWS)
        cv = cv[:, :dec_seq].reshape(1, n_dec, cw)
        xs_new = _out(lidx, a, cv, gm, xs2, mod_s, mod_rows_s, w_ao, w_co, w_o, n_dec).reshape(xs.shape)
        past_state = lax.dynamic_index_in_dim(state_conv, l, 0, keepdims=False)
        new_conv_s = jnp.concatenate([past_state, u_r], axis=1)[:, -(CONV_K - 1):]
        outs_s = (k32.reshape(dec_batch, dec_seq, N_KV_HEADS, HEAD_DIM),
                  v32.reshape(dec_batch, dec_seq, N_KV_HEADS, HEAD_DIM),
                  ki32.reshape(dec_batch, dec_seq, IDX_DIM), new_conv_s)
        return (xp_new, xs_new), outs_p + outs_s

    (yp, ys), per_layer = lax.scan(layer, (x_prompt, x_sample), jnp.arange(n_layers, dtype=jnp.int32))
    return (yp, ys) + tuple(per_layer)
```

```python
import functools

import jax
import jax.numpy as jnp
from jax import lax
from jax.experimental import pallas as pl
from jax.experimental.pallas import tpu as pltpu

F32 = jnp.float32
BF16 = jnp.bfloat16
I32 = jnp.int32

N_HEADS = 8
HEAD_DIM = 128
N_KV_HEADS = 2
HEADS_PER_KV = N_HEADS // N_KV_HEADS
N_IDX_HEADS = 16
IDX_DIM = 64
TOPK_MAX = 256
CONV_K = 31
ROPE_THETA = 500000.0
EPS = 1e-6

LANE = 128
ATT_W = N_HEADS * HEAD_DIM
KV_W = N_KV_HEADS * HEAD_DIM
QI_W = N_IDX_HEADS * IDX_DIM
CONV_HALO = 32
SAMPLE_ROWS = 16
KEY_NEG_INF = -2139095041
KEY_POS_INF = 0x7F800000
VALUE_PASSES = 24
PASSES_PER_TRIP = 4
MAX_PASSES = VALUE_PASSES + 32
NEG_BIG = -1e30
LOG2E = 1.4426950408889634
VMEM_LIMIT = 56 * 1024 * 1024

_NT = (((1,), (1,)), ((), ()))


def _params(sem):
    return pltpu.CompilerParams(dimension_semantics=sem, vmem_limit_bytes=VMEM_LIMIT)


def _resident(block, imap):
    return pl.BlockSpec(block, imap, pipeline_mode=pl.Buffered(1))


def _silu(x):
    return x * jax.nn.sigmoid(x)


def _mod_kernel(c_ref, w_ref, b_ref, o_ref):
    a = _silu(c_ref[...]).astype(BF16)
    o_ref[0] = jnp.dot(a, w_ref[0].astype(BF16), preferred_element_type=F32) + b_ref[0]


def _modulation(c_all, w_ada, b_ada):
    n_layers, d, n3 = w_ada.shape
    rows = c_all.shape[0]
    tn = next(w for w in (512, 256, LANE, n3) if n3 % w == 0)
    return pl.pallas_call(
        _mod_kernel,
        grid=(n_layers, n3 // tn),
        in_specs=[pl.BlockSpec((rows, d), lambda l, j: (0, 0)),
                  pl.BlockSpec((1, d, tn), lambda l, j: (l, 0, j)),
                  pl.BlockSpec((1, 1, tn), lambda l, j: (l, 0, j))],
        out_specs=pl.BlockSpec((1, rows, tn), lambda l, j: (l, 0, j)),
        out_shape=jax.ShapeDtypeStruct((n_layers, rows, n3), F32),
        compiler_params=_params(("parallel", "parallel")),
        name="adaln_mod",
    )(c_all, w_ada, b_ada.reshape(n_layers, 1, n3))


def _norm_kernel(l_ref, x_ref, g_ref, sc_ref, sh_ref, o_ref):
    x = x_ref[0]
    y = x * lax.rsqrt(jnp.mean(x * x, axis=-1, keepdims=True) + EPS) * g_ref[0]
    o_ref[0] = (y * (1.0 + sc_ref[0]) + sh_ref[0]).astype(BF16)


def _norm(lidx, x, norm_g, mod, mod_rows, tm):
    nb, t, d = x.shape
    mod_imap = lambda part: (lambda b, i, l: (mod_rows(l[0], b), 0, part))
    return pl.pallas_call(
        _norm_kernel,
        grid_spec=pltpu.PrefetchScalarGridSpec(
            num_scalar_prefetch=1, grid=(nb, t // tm),
            in_specs=[pl.BlockSpec((1, tm, d), lambda b, i, l: (b, i, 0)),
                      pl.BlockSpec((1, 1, d), lambda b, i, l: (l[0], 0, 0)),
                      pl.BlockSpec((1, mod.shape[1], d), mod_imap(1)),
                      pl.BlockSpec((1, mod.shape[1], d), mod_imap(0))],
            out_specs=pl.BlockSpec((1, tm, d), lambda b, i, l: (b, i, 0))),
        out_shape=jax.ShapeDtypeStruct((nb, t, d), BF16),
        compiler_params=_params(("parallel", "parallel")),
        name="adaln_rmsnorm",
    )(lidx, x, norm_g, mod, mod)


def _rope(n, c, s1, s2, half):
    return n * c + pltpu.roll(n, LANE - half, 1) * s1 + pltpu.roll(n, half, 1) * s2


def _attn_in_kernel(l_ref, h_ref, w_ref, wkw_ref, qg_ref, kg_ref, gi_ref, rc_ref, rs1_ref, rs2_ref,
                    ic_ref, is1_ref, is2_ref,
                    q_ref, k32_ref, kbf_ref, v32_ref, vbf_ref, qi_ref, ki32_ref, kibf_ref, wi_ref, *, v_transposed):
    h = h_ref[0]
    rc, rs1, rs2 = rc_ref[...], rs1_ref[...], rs2_ref[...]
    ic, is1, is2 = ic_ref[...], is1_ref[...], is2_ref[...]

    def proj(c0, width):
        return jnp.dot(h, w_ref[0, :, c0:c0 + width], preferred_element_type=F32)

    def head_norm(z, g):
        return z * lax.rsqrt(jnp.mean(z * z, axis=-1, keepdims=True) + EPS) * g

    scale = HEAD_DIM ** -0.5 * LOG2E
    for pair in range(N_HEADS // 2):
        z2 = proj(pair * 2 * HEAD_DIM, 2 * HEAD_DIM)
        for j in range(2):
            z = z2[:, j * HEAD_DIM:(j + 1) * HEAD_DIM]
            r = _rope(head_norm(z, qg_ref[0]), rc, rs1, rs2, HEAD_DIM // 8)
            q_ref[0, 2 * pair + j] = (r * scale).astype(BF16)
    z2 = proj(ATT_W, KV_W)
    for g in range(N_KV_HEADS):
        z = z2[:, g * HEAD_DIM:(g + 1) * HEAD_DIM]
        r = _rope(head_norm(z, kg_ref[0]), rc, rs1, rs2, HEAD_DIM // 8)
        k32_ref[0, :, g * HEAD_DIM:(g + 1) * HEAD_DIM] = r
        kbf_ref[0, :, g * HEAD_DIM:(g + 1) * HEAD_DIM] = r.astype(BF16)
    z2 = proj(ATT_W + KV_W, KV_W)
    v32_ref[0] = z2
    if v_transposed:
        vbf_ref[0, 0] = z2.T.astype(BF16)
    else:
        vbf_ref[0] = z2.astype(BF16)

    lane = lax.broadcasted_iota(I32, (1, LANE), 1)
    lo = (lane < IDX_DIM).astype(F32)
    o_qi = ATT_W + 2 * KV_W
    for quad in range(N_IDX_HEADS // 4):
        z4 = proj(o_qi + quad * 2 * LANE, 2 * LANE)
        for j in range(2):
            r = _rope(z4[:, j * LANE:(j + 1) * LANE], ic, is1, is2, IDX_DIM // 8)
            qi_ref[0, 4 * quad + 2 * j] = (r * lo).astype(BF16)
            qi_ref[0, 4 * quad + 2 * j + 1] = (pltpu.roll(r, IDX_DIM, 1) * lo).astype(BF16)
    z = jnp.dot(h, wkw_ref[0], preferred_element_type=F32)
    ss = jnp.sum(z * z * lo, axis=-1, keepdims=True) * (1.0 / IDX_DIM)
    n = z * lax.rsqrt(ss + EPS) * gi_ref[0]
    r = _rope(n, ic, is1, is2, IDX_DIM // 8)
    ki32_ref[0] = r[:, :IDX_DIM]
    kibf_ref[0] = r.astype(BF16)
    wi_ref[0] = pltpu.roll(z, LANE - IDX_DIM, 1) * (N_IDX_HEADS ** -0.5 * IDX_DIM ** -0.5)


def _attn_in(lidx, h, w_a, w_kw, qg, kg, gi, rope_tabs, idx_tabs, tm, v_transposed):
    nb, t, d = h.shape
    na = w_a.shape[-1]
    row = lambda b, i, l: (b, i, 0)
    hm = lambda b, i, l: (b, 0, i, 0)
    lay = lambda b, i, l: (l[0], 0, 0)
    tab = pl.BlockSpec((tm, LANE), lambda b, i, l: (i, 0))
    if v_transposed:
        v_out = (jax.ShapeDtypeStruct((nb, t // tm, KV_W, tm), BF16),
                 pl.BlockSpec((1, 1, KV_W, tm), lambda b, i, l: (b, i, 0, 0)))
    else:
        v_out = (jax.ShapeDtypeStruct((nb, t, KV_W), BF16), pl.BlockSpec((1, tm, KV_W), row))
    outs = [
        (jax.ShapeDtypeStruct((nb, N_HEADS, t, HEAD_DIM), BF16), pl.BlockSpec((1, N_HEADS, tm, HEAD_DIM), hm)),
        (jax.ShapeDtypeStruct((nb, t, KV_W), F32), pl.BlockSpec((1, tm, KV_W), row)),
        (jax.ShapeDtypeStruct((nb, t, KV_W), BF16), pl.BlockSpec((1, tm, KV_W), row)),
        (jax.ShapeDtypeStruct((nb, t, KV_W), F32), pl.BlockSpec((1, tm, KV_W), row)),
        v_out,
        (jax.ShapeDtypeStruct((nb, N_IDX_HEADS, t, LANE), BF16), pl.BlockSpec((1, N_IDX_HEADS, tm, LANE), hm)),
        (jax.ShapeDtypeStruct((nb, t, IDX_DIM), F32), pl.BlockSpec((1, tm, IDX_DIM), row)),
        (jax.ShapeDtypeStruct((nb, t, LANE), BF16), pl.BlockSpec((1, tm, LANE), row)),
        (jax.ShapeDtypeStruct((nb, t, LANE), F32), pl.BlockSpec((1, tm, LANE), row)),
    ]
    return pl.pallas_call(
        functools.partial(_attn_in_kernel, v_transposed=v_transposed),
        grid_spec=pltpu.PrefetchScalarGridSpec(
            num_scalar_prefetch=1, grid=(nb, t // tm),
            in_specs=[pl.BlockSpec((1, tm, d), row),
                      _resident((1, d, na), lay), _resident((1, d, LANE), lay),
                      pl.BlockSpec((1, 1, LANE), lay), pl.BlockSpec((1, 1, LANE), lay),
                      pl.BlockSpec((1, 1, LANE), lay),
                      tab, tab, tab, tab, tab, tab],
            out_specs=[o[1] for o in outs]),
        out_shape=[o[0] for o in outs],
        compiler_params=_params(("parallel", "parallel")),
        name="attn_in_proj",
    )(lidx, h, w_a, w_kw, qg, kg, gi, *rope_tabs, *idx_tabs)


def _conv_in_kernel(l_ref, h_ref, w_ref, u_ref, sgc_ref, *, cw, bw):
    h = h_ref[0]
    for c0 in range(0, cw, bw):
        za = jnp.dot(h, w_ref[0, :, c0:c0 + bw], preferred_element_type=F32)
        zb = jnp.dot(h, w_ref[0, :, cw + c0:cw + c0 + bw], preferred_element_type=F32)
        u_ref[0, :, c0:c0 + bw] = za * jax.nn.sigmoid(zb)
        zg = jnp.dot(h, w_ref[0, :, 2 * cw + c0:2 * cw + c0 + bw], preferred_element_type=F32)
        sgc_ref[0, :, c0:c0 + bw] = _silu(zg).astype(BF16)


def _conv_in(lidx, h, w_b, tm):
    nb, t, d = h.shape
    cw = w_b.shape[-1] // 3
    row = lambda b, i, l: (b, i, 0)
    return pl.pallas_call(
        functools.partial(_conv_in_kernel, cw=cw, bw=min(256, cw)),
        grid_spec=pltpu.PrefetchScalarGridSpec(
            num_scalar_prefetch=1, grid=(nb, t // tm),
            in_specs=[pl.BlockSpec((1, tm, d), row),
                      _resident((1, d, 3 * cw), lambda b, i, l: (l[0], 0, 0))],
            out_specs=[pl.BlockSpec((1, tm, cw), row), pl.BlockSpec((1, tm, cw), row)]),
        out_shape=[jax.ShapeDtypeStruct((nb, t, cw), F32), jax.ShapeDtypeStruct((nb, t, cw), BF16)],
        compiler_params=_params(("parallel", "parallel")),
        name="conv_in_proj",
    )(lidx, h, w_b)


def _gate_in_kernel(l_ref, h_ref, wga_ref, wmg_ref, sga_ref, gm_ref, *, bw):
    h = h_ref[0]
    for c0 in range(0, ATT_W, bw):
        z = jnp.dot(h, wga_ref[0, :, c0:c0 + bw], preferred_element_type=F32)
        sga_ref[0, :, c0:c0 + bw] = _silu(z).astype(BF16)
    for c0 in range(0, gm_ref.shape[-1], bw):
        z = jnp.dot(h, wmg_ref[0, :, c0:c0 + bw], preferred_element_type=F32)
        gm_ref[0, :, c0:c0 + bw] = jax.nn.sigmoid(z).astype(BF16)


def _gate_in(lidx, h, w_ga, w_mg, tm):
    nb, t, d = h.shape
    nm = w_mg.shape[-1]
    row = lambda b, i, l: (b, i, 0)
    lay = lambda b, i, l: (l[0], 0, 0)
    return pl.pallas_call(
        functools.partial(_gate_in_kernel, bw=256),
        grid_spec=pltpu.PrefetchScalarGridSpec(
            num_scalar_prefetch=1, grid=(nb, t // tm),
            in_specs=[pl.BlockSpec((1, tm, d), row), _resident((1, d, ATT_W), lay), _resident((1, d, nm), lay)],
            out_specs=[pl.BlockSpec((1, tm, ATT_W), row), pl.BlockSpec((1, tm, nm), row)]),
        out_shape=[jax.ShapeDtypeStruct((nb, t, ATT_W), BF16), jax.ShapeDtypeStruct((nb, t, nm), BF16)],
        compiler_params=_params(("parallel", "parallel")),
        name="gate_in_proj",
    )(lidx, h, w_ga, w_mg)


def _key_to_float(key):
    b = key ^ ((key >> 31) & jnp.int32(0x7FFFFFFF))
    return lax.bitcast_convert_type(b, F32)


def _float_to_key(x):
    b = lax.bitcast_convert_type(x, I32)
    return b ^ ((b >> 31) & jnp.int32(0x7FFFFFFF))


def _kth_largest(count_ge, lo_f, hi_f, n_ge_lo, k):
    kf = jnp.float32(k)
    select_all = n_ge_lo <= kf
    lo0 = jnp.where(select_all, jnp.int32(KEY_NEG_INF + 1), _float_to_key(lo_f))
    hi0 = _float_to_key(hi_f) + 1

    def one_pass(it, lo, hi, clo, active):
        lo_v = _key_to_float(lo)
        hi_v = _key_to_float(jnp.minimum(hi, jnp.int32(KEY_POS_INF)))
        mid_value = _float_to_key(0.5 * lo_v + 0.5 * hi_v)
        mid_image = (lo & hi) + ((lo ^ hi) >> 1)
        cand = jnp.where(it < VALUE_PASSES, mid_value, mid_image)
        cand = jnp.minimum(jnp.maximum(cand, lo + 1), hi - 1)
        cnt = count_ge(_key_to_float(cand))
        up = (cnt >= kf) & (active > 0.0)
        down = (cnt < kf) & (active > 0.0)
        lo = jnp.where(up, cand, lo)
        clo = jnp.where(up, cnt, clo)
        hi = jnp.where(down, cand, hi)
        active = jnp.where((clo == kf) | (hi <= lo + 1), 0.0, active)
        return lo, hi, clo, active

    def cond(st):
        return (st[0] < MAX_PASSES) & (jnp.sum(st[4]) > 0.0)

    def body(st):
        it, lo, hi, clo, active = st
        for sub in range(PASSES_PER_TRIP):
            lo, hi, clo, active = one_pass(it + sub, lo, hi, clo, active)
        return it + PASSES_PER_TRIP, lo, hi, clo, active

    active0 = jnp.where(select_all, 0.0, 1.0)
    st = lax.while_loop(cond, body, (jnp.int32(0), lo0, hi0, n_ge_lo, active0))
    thr = _key_to_float(jnp.maximum(st[1], jnp.int32(KEY_NEG_INF + 1)))
    return thr, st[3] - kf


def _tie_cutoff(tie_count, surplus, n_keys):
    keep = tie_count(jnp.full(surplus.shape, n_keys, I32)) - surplus

    def body(_, st):
        lo, hi = st
        mid = lo + ((hi - lo) >> 1)
        ok = tie_count(mid) >= keep
        return jnp.where(ok, lo, mid), jnp.where(ok, mid, hi)

    lo0 = jnp.full(surplus.shape, -1, I32)
    hi0 = jnp.full(surplus.shape, n_keys - 1, I32)
    _, hi = lax.fori_loop(0, max(n_keys - 1, 1).bit_length() + 1, body, (lo0, hi0))
    return jnp.where(surplus > 0.0, hi, jnp.int32(n_keys))


def _fold_rows(x, op):
    r, n = x.shape
    if r > 64:
        x = op(x.reshape(r // 64, 64, n), axis=0)
    return op(x.reshape(x.shape[0] // 8, 8, n), axis=0)


def _index_scores(qi, wi, kc, rows):
    sc = jnp.zeros((rows, kc.shape[0]), F32)
    for hg in range(N_IDX_HEADS // 4):
        d = lax.dot_general(qi[hg * 4 * rows:(hg + 1) * 4 * rows], kc, _NT, preferred_element_type=F32)
        for j in range(4):
            hd = hg * 4 + j
            sc = sc + jnp.maximum(d[j * rows:(j + 1) * rows], 0.0) * wi[:, hd:hd + 1]
    return sc


def _softmax_step(s, vc, mask, m_ref, l_ref, acc_ref, rows):
    ck = vc.shape[0]
    s = jnp.where(mask[None], s.reshape(HEADS_PER_KV, rows, ck), NEG_BIG).reshape(HEADS_PER_KV * rows, ck)
    m_prev = m_ref[:, :1]
    m_new = jnp.maximum(m_prev, jnp.max(s, axis=-1, keepdims=True))
    alpha = jnp.exp2(m_prev - m_new)
    p = jnp.exp2(s - m_new)
    l_ref[...] = jnp.broadcast_to(alpha * l_ref[:, :1] + jnp.sum(p, axis=-1, keepdims=True), l_ref.shape)
    acc_ref[...] = alpha * acc_ref[...] + jnp.dot(p.astype(BF16), vc, preferred_element_type=F32)
    m_ref[...] = jnp.broadcast_to(m_new, m_ref.shape)


def _softmax_init(m_ref, l_ref, acc_ref):
    m_ref[...] = jnp.full(m_ref.shape, NEG_BIG, F32)
    l_ref[...] = jnp.zeros(l_ref.shape, F32)
    acc_ref[...] = jnp.zeros(acc_ref.shape, F32)


def _softmax_finish(g, sga_ref, o_ref, l_ref, acc_ref, rows):
    o = acc_ref[...] / l_ref[:, :1]
    for j in range(HEADS_PER_KV):
        c0 = (g * HEADS_PER_KV + j) * HEAD_DIM
        gate = sga_ref[0, :, c0:c0 + HEAD_DIM].astype(F32)
        o_ref[0, :, c0:c0 + HEAD_DIM] = (o[j * rows:(j + 1) * rows] * gate).astype(BF16)


def _prompt_attn_kernel(q_ref, qi_ref, wi_ref, k_ref, vt_ref, ki_ref, sga_ref, o_ref,
                        sc_s, m_s, l_s, acc_s, *, tq, ck, topk):
    row0 = pl.program_id(1) * tq
    nch = (row0 + tq + ck - 1) // ck
    wit = wi_ref[0].T
    kpos = lax.broadcasted_iota(I32, (ck, tq), 0)
    qpos = row0 + lax.broadcasted_iota(I32, (ck, tq), 1)

    def score_chunk(c, carry):
        kc = ki_ref[0, pl.ds(pl.multiple_of(c * ck, ck), ck), :]
        sc = jnp.zeros((ck, tq), F32)
        for hd in range(N_IDX_HEADS):
            d = lax.dot_general(kc, qi_ref[0, hd], _NT, preferred_element_type=F32)
            sc = sc + jnp.maximum(d, 0.0) * wit[hd:hd + 1, :]
        sc_s[c] = jnp.where(kpos + c * ck <= qpos, sc, -jnp.inf)
        return carry

    lax.fori_loop(0, nch, score_chunk, 0)

    def count_ge(thr):
        def body(c, acc):
            return acc + _fold_rows(jnp.where(sc_s[c] >= thr, 1.0, 0.0), jnp.sum)

        acc = lax.fori_loop(0, nch, body, jnp.zeros((8, tq), F32))
        return jnp.sum(acc, axis=0, keepdims=True)

    def fold_minmax(s, carry, masked):
        mx, mn = carry
        mx = jnp.maximum(mx, _fold_rows(s, jnp.max))
        if masked:
            s = jnp.where(s == -jnp.inf, jnp.inf, s)
        return mx, jnp.minimum(mn, _fold_rows(s, jnp.min))

    init = (jnp.full((8, tq), -jnp.inf, F32), jnp.full((8, tq), jnp.inf, F32))
    mm = lax.fori_loop(0, nch - 1, lambda c, carry: fold_minmax(sc_s[c], carry, False), init)
    mx, mn = fold_minmax(sc_s[nch - 1], mm, True)
    n_visible = (qpos[:1] + 1).astype(F32)
    thr, surplus = _kth_largest(count_ge, jnp.min(mn, axis=0, keepdims=True), jnp.max(mx, axis=0, keepdims=True),
                                n_visible, topk)

    @pl.when(jnp.max(surplus) > 0.0)
    def _():
        def tie_count(jcut):
            def body(c, acc):
                hit = (sc_s[c] == thr) & (kpos + c * ck <= jcut)
                return acc + _fold_rows(jnp.where(hit, 1.0, 0.0), jnp.sum)

            acc = lax.fori_loop(0, nch, body, jnp.zeros((8, tq), F32))
            return jnp.sum(acc, axis=0, keepdims=True)

        jcut = _tie_cutoff(tie_count, surplus, sc_s.shape[0] * ck)

        def hide(c, carry):
            s = sc_s[c]
            sc_s[c] = jnp.where((s == thr) & (kpos + c * ck > jcut), -jnp.inf, s)
            return carry

        lax.fori_loop(0, nch, hide, 0)

    _softmax_init(m_s, l_s, acc_s)

    def attend_chunk(c, carry):
        bias = jnp.where(sc_s[c] >= thr, 0.0, NEG_BIG)
        off = pl.multiple_of(c * ck, ck)

        def qk(g):
            kc = k_ref[0, pl.ds(off, ck), g * HEAD_DIM:(g + 1) * HEAD_DIM]
            qg = q_ref[0, g * HEADS_PER_KV:(g + 1) * HEADS_PER_KV].reshape(HEADS_PER_KV * tq, HEAD_DIM)
            return lax.dot_general(kc, qg, _NT, preferred_element_type=F32)

        logits = [qk(g) for g in range(N_KV_HEADS)]
        for g in range(N_KV_HEADS):
            ps, alphas = [], []
            for j in range(HEADS_PER_KV):
                hd = g * HEADS_PER_KV + j
                s = logits[g][:, j * tq:(j + 1) * tq] + bias
                m_prev = m_s[hd]
                m_new = jnp.maximum(m_prev, jnp.max(_fold_rows(s, jnp.max), axis=0, keepdims=True))
                alpha = jnp.exp2(m_prev - m_new)
                p = jnp.exp2(s - m_new)
                l_s[hd] = alpha * l_s[hd] + jnp.sum(_fold_rows(p, jnp.sum), axis=0, keepdims=True)
                m_s[hd] = m_new
                ps.append(p.astype(BF16))
                alphas.append(alpha)
            vtc = vt_ref[0, c, g * HEAD_DIM:(g + 1) * HEAD_DIM, :]
            pv = jnp.dot(vtc, jnp.concatenate(ps, axis=1), preferred_element_type=F32)
            acc_s[g] = jnp.concatenate(alphas, axis=1) * acc_s[g] + pv
        return carry

    lax.fori_loop(0, nch, attend_chunk, 0)
    for hd in range(N_HEADS):
        g, j = divmod(hd, HEADS_PER_KV)
        o = (acc_s[g, :, j * tq:(j + 1) * tq] / l_s[hd]).T
        gate = sga_ref[0, :, hd * HEAD_DIM:(hd + 1) * HEAD_DIM].astype(F32)
        o_ref[0, :, hd * HEAD_DIM:(hd + 1) * HEAD_DIM] = (o * gate).astype(BF16)


def _prompt_attention(q, qi, wi, kbf, vt, kibf, sga, topk, tq, ck):
    nb, _, t, _ = q.shape
    assert vt.shape == (nb, t // ck, KV_W, ck)
    full = lambda b, i: (b, 0, 0)
    row = lambda b, i: (b, i, 0)
    hm = lambda b, i: (b, 0, i, 0)
    return pl.pallas_call(
        functools.partial(_prompt_attn_kernel, tq=tq, ck=ck, topk=topk),
        grid=(nb, t // tq),
        in_specs=[pl.BlockSpec((1, N_HEADS, tq, HEAD_DIM), hm),
                  pl.BlockSpec((1, N_IDX_HEADS, tq, LANE), hm),
                  pl.BlockSpec((1, tq, LANE), row),
                  pl.BlockSpec((1, t, KV_W), full),
                  pl.BlockSpec((1, t // ck, KV_W, ck), lambda b, i: (b, 0, 0, 0)),
                  pl.BlockSpec((1, t, LANE), full),
                  pl.BlockSpec((1, tq, ATT_W), row)],
        out_specs=pl.BlockSpec((1, tq, ATT_W), row),
        out_shape=jax.ShapeDtypeStruct((nb, t, ATT_W), BF16),
        scratch_shapes=[pltpu.VMEM((t // ck, ck, tq), F32),
                        pltpu.VMEM((N_HEADS, 1, tq), F32),
                        pltpu.VMEM((N_HEADS, 1, tq), F32),
                        pltpu.VMEM((N_KV_HEADS, HEAD_DIM, HEADS_PER_KV * tq), F32)],
        compiler_params=_params(("parallel", "arbitrary")),
        name="prompt_sparse_attention",
    )(q, qi, wi, kbf, vt, kibf, sga)


def _paged_fetch(l_ref, pt_ref, hbm_refs, raw_refs, sems, *, pages, n_phys):
    b, j = pl.program_id(0), pl.program_id(1)
    nj = pl.num_programs(1)
    step = b * nj + j
    slot = step % 2

    def copies(bb, jj, sl):
        out = []
        for p in range(pages):
            row = l_ref[0] * n_phys + pt_ref[bb, jj * pages + p]
            for a, (hbm, raw) in enumerate(zip(hbm_refs, raw_refs)):
                n = hbm.shape[1]
                out.append(pltpu.make_async_copy(hbm.at[row], raw.at[sl, pl.ds(p * n, n)], sems.at[a, sl]))
        return out

    @pl.when(step == 0)
    def _():
        for cp in copies(b, j, slot):
            cp.start()

    @pl.when(step + 1 < pl.num_programs(0) * nj)
    def _():
        for cp in copies((step + 1) // nj, (step + 1) % nj, 1 - slot):
            cp.start()

    for cp in copies(b, j, slot):
        cp.wait()
    return slot


def _sample_scores_kernel(l_ref, pt_ref, qi_ref, wi_ref, cache_hbm, o_ref, raw, sems, *, pages, n_phys):
    slot = _paged_fetch(l_ref, pt_ref, [cache_hbm], [raw], sems, pages=pages, n_phys=n_phys)
    qi = qi_ref[0].reshape(N_IDX_HEADS * SAMPLE_ROWS, LANE)[:, :IDX_DIM]
    o_ref[0] = _index_scores(qi, wi_ref[0], raw[slot].astype(BF16), SAMPLE_ROWS)


def _sample_scores(lidx, page_table, qi, wi, cache_kidx, n_phys, pages):
    nb, n_pages = page_table.shape
    page = cache_kidx.shape[1]
    return pl.pallas_call(
        functools.partial(_sample_scores_kernel, pages=pages, n_phys=n_phys),
        grid_spec=pltpu.PrefetchScalarGridSpec(
            num_scalar_prefetch=2, grid=(nb, n_pages // pages),
            in_specs=[pl.BlockSpec((1, N_IDX_HEADS, SAMPLE_ROWS, LANE), lambda b, j, l, pt: (b, 0, 0, 0)),
                      pl.BlockSpec((1, SAMPLE_ROWS, LANE), lambda b, j, l, pt: (b, 0, 0)),
                      pl.BlockSpec(memory_space=pl.ANY)],
            out_specs=pl.BlockSpec((1, SAMPLE_ROWS, pages * page), lambda b, j, l, pt: (b, 0, j)),
            scratch_shapes=[pltpu.VMEM((2, pages * page, IDX_DIM), F32),
                            pltpu.SemaphoreType.DMA((1, 2))]),
        out_shape=jax.ShapeDtypeStruct((nb, SAMPLE_ROWS, n_pages * page), F32),
        compiler_params=_params(("arbitrary", "arbitrary")),
        name="sample_index_scores",
    )(lidx, page_table, qi, wi, cache_kidx)


def _sample_threshold_kernel(sc_ref, qi_ref, wi_ref, kin_ref, thr_ref, newsc_ref, jcut_ref, *, topk, n_new):
    rows = SAMPLE_ROWS
    n_past = sc_ref.shape[-1]
    qi = qi_ref[0].reshape(N_IDX_HEADS * rows, LANE)
    sc_new = _index_scores(qi, wi_ref[0], kin_ref[0], rows)
    r = lax.broadcasted_iota(I32, sc_new.shape, 0)
    c = lax.broadcasted_iota(I32, sc_new.shape, 1)
    sc_new = jnp.where((c <= r) & (c < n_new), sc_new, -jnp.inf)
    newsc_ref[0] = sc_new

    def count_rows(hit_new, hit_past):
        accs = [jnp.where(hit_new, 1.0, 0.0)] + [jnp.zeros((rows, LANE), F32)] * 7
        for j in range(n_past // LANE):
            accs[j % 8] = accs[j % 8] + jnp.where(hit_past(sc_ref[0, :, j * LANE:(j + 1) * LANE], j), 1.0, 0.0)
        acc = (accs[0] + accs[1]) + (accs[2] + accs[3]) + ((accs[4] + accs[5]) + (accs[6] + accs[7]))
        return jnp.sum(acc, axis=-1, keepdims=True)

    def count_ge(thr):
        tb = jnp.broadcast_to(thr, (rows, LANE))
        return count_rows(sc_new >= tb, lambda s, j: s >= tb)

    past = sc_ref[0]
    mx = jnp.maximum(jnp.max(past, axis=-1, keepdims=True), jnp.max(sc_new, axis=-1, keepdims=True))
    mn = jnp.minimum(jnp.min(past, axis=-1, keepdims=True),
                     jnp.min(jnp.where(sc_new == -jnp.inf, jnp.inf, sc_new), axis=-1, keepdims=True))
    n_visible = (n_past + jnp.minimum(r[:, :1] + 1, n_new)).astype(F32)
    thr, surplus = _kth_largest(count_ge, mn, mx, n_visible, topk)
    thr_ref[0] = jnp.broadcast_to(thr, (rows, LANE))
    n_keys = n_past + LANE
    jcut_ref[0] = jnp.full((rows, LANE), n_keys, I32)

    @pl.when(jnp.max(surplus) > 0.0)
    def _():
        tb = jnp.broadcast_to(thr, (rows, LANE))

        def tie_count(jcut):
            jb = jnp.broadcast_to(jcut, (rows, LANE))
            return count_rows((sc_new == tb) & (n_past + c <= jb), lambda s, j: (s == tb) & (j * LANE + c <= jb))

        jcut_ref[0] = jnp.broadcast_to(_tie_cutoff(tie_count, surplus, n_keys), (rows, LANE))


def _sample_threshold(scores, qi, wi, ki_new, topk, n_new):
    nb, rows, n_keys = scores.shape
    b3 = lambda b: (b, 0, 0)
    return pl.pallas_call(
        functools.partial(_sample_threshold_kernel, topk=topk, n_new=n_new),
        grid=(nb,),
        in_specs=[pl.BlockSpec((1, rows, n_keys), b3),
                  pl.BlockSpec((1, N_IDX_HEADS, rows, LANE), lambda b: (b, 0, 0, 0)),
                  pl.BlockSpec((1, rows, LANE), b3),
                  pl.BlockSpec((1, LANE, LANE), b3)],
        out_specs=[pl.BlockSpec((1, rows, LANE), b3)] * 3,
        out_shape=[jax.ShapeDtypeStruct((nb, rows, LANE), F32), jax.ShapeDtypeStruct((nb, rows, LANE), F32),
                   jax.ShapeDtypeStruct((nb, rows, LANE), I32)],
        compiler_params=_params(("parallel",)),
        name="sample_topk_threshold",
    )(scores, qi, wi, ki_new)


def _sample_attn_kernel(l_ref, pt_ref, q_ref, sc_ref, thr_ref, newsc_ref, jcut_ref, knew_ref, vnew_ref, sga_ref,
                        ck_hbm, cv_hbm, o_ref, kraw, vraw, sems, m_s, l_s, acc_s, *, pages, n_phys):
    slot = _paged_fetch(l_ref, pt_ref, [ck_hbm, cv_hbm], [kraw, vraw], sems, pages=pages, n_phys=n_phys)
    rows = SAMPLE_ROWS
    n_chunk = kraw.shape[1] // N_KV_HEADS
    j = pl.program_id(1)
    thr = thr_ref[0][:, :1]
    jcut = jcut_ref[0][:, :1]

    @pl.when(j == 0)
    def _():
        for g in range(N_KV_HEADS):
            _softmax_init(m_s.at[g], l_s.at[g], acc_s.at[g])

    def q_group(g):
        return q_ref[0, g * HEADS_PER_KV:(g + 1) * HEADS_PER_KV].reshape(HEADS_PER_KV * rows, HEAD_DIM)

    def selected(s, first_index):
        idx = first_index + lax.broadcasted_iota(I32, s.shape, 1)
        return (s > thr) | ((s == thr) & (idx <= jcut))

    def head_rows(raw, g):
        return raw[slot, pl.ds(g, n_chunk, stride=N_KV_HEADS), :].astype(BF16)

    mask = selected(sc_ref[0], j * n_chunk)
    logits = [lax.dot_general(q_group(g), head_rows(kraw, g), _NT, preferred_element_type=F32)
              for g in range(N_KV_HEADS)]
    for g in range(N_KV_HEADS):
        _softmax_step(logits[g], head_rows(vraw, g), mask, m_s.at[g], l_s.at[g], acc_s.at[g], rows)

    @pl.when(j == pl.num_programs(1) - 1)
    def _():
        mask_new = selected(newsc_ref[0], pl.num_programs(1) * n_chunk)
        for g in range(N_KV_HEADS):
            kc = knew_ref[0, :, g * HEAD_DIM:(g + 1) * HEAD_DIM]
            vc = vnew_ref[0, :, g * HEAD_DIM:(g + 1) * HEAD_DIM]
            s = lax.dot_general(q_group(g), kc, _NT, preferred_element_type=F32)
            _softmax_step(s, vc, mask_new, m_s.at[g], l_s.at[g], acc_s.at[g], rows)
            _softmax_finish(g, sga_ref, o_ref, l_s.at[g], acc_s.at[g], rows)


def _sample_attention(lidx, page_table, q, scores, thr, newsc, jcut, k_new, v_new, sga, cache_k, cache_v, n_phys,
                      pages):
    nb, n_pages = page_table.shape
    page = cache_k.shape[1] // N_KV_HEADS
    rows = SAMPLE_ROWS
    b3 = lambda b, j, l, pt: (b, 0, 0)
    return pl.pallas_call(
        functools.partial(_sample_attn_kernel, pages=pages, n_phys=n_phys),
        grid_spec=pltpu.PrefetchScalarGridSpec(
            num_scalar_prefetch=2, grid=(nb, n_pages // pages),
            in_specs=[pl.BlockSpec((1, N_HEADS, rows, HEAD_DIM), lambda b, j, l, pt: (b, 0, 0, 0)),
                      pl.BlockSpec((1, rows, pages * page), lambda b, j, l, pt: (b, 0, j)),
                      pl.BlockSpec((1, rows, LANE), b3), pl.BlockSpec((1, rows, LANE), b3),
                      pl.BlockSpec((1, rows, LANE), b3),
                      pl.BlockSpec((1, LANE, KV_W), b3), pl.BlockSpec((1, LANE, KV_W), b3),
                      pl.BlockSpec((1, rows, ATT_W), b3),
                      pl.BlockSpec(memory_space=pl.ANY), pl.BlockSpec(memory_space=pl.ANY)],
            out_specs=pl.BlockSpec((1, rows, ATT_W), b3),
            scratch_shapes=[pltpu.VMEM((2, pages * page * N_KV_HEADS, HEAD_DIM), F32),
                            pltpu.VMEM((2, pages * page * N_KV_HEADS, HEAD_DIM), F32),
                            pltpu.SemaphoreType.DMA((2, 2)),
                            pltpu.VMEM((N_KV_HEADS, HEADS_PER_KV * rows, LANE), F32),
                            pltpu.VMEM((N_KV_HEADS, HEADS_PER_KV * rows, LANE), F32),
                            pltpu.VMEM((N_KV_HEADS, HEADS_PER_KV * rows, HEAD_DIM), F32)]),
        out_shape=jax.ShapeDtypeStruct((nb, rows, ATT_W), BF16),
        compiler_params=_params(("arbitrary", "arbitrary")),
        name="sample_sparse_attention",
    )(lidx, page_table, q, scores, thr, newsc, jcut, k_new, v_new, sga, cache_k, cache_v)


def _conv_kernel(l_ref, u_ref, st_ref, w_ref, b_ref, g_ref, beta_ref, sgc_ref, o_ref, ext_s, sh_s, d_s,
                 *, tt, rb, cb):
    cw = u_ref.shape[-1]

    @pl.when(pl.program_id(1) == 0)
    def _():
        ext_s[0:CONV_HALO] = st_ref[0]

    ext_s[CONV_HALO:CONV_HALO + tt] = u_ref[0]
    first = CONV_HALO - (CONV_K - 1)
    n_sh = sh_s.shape[1]
    for ph in range(1, 8):
        sh_s[ph - 1] = ext_s[ph:ph + n_sh, :]
    for c0 in range(0, cw, cb):
        for r0 in range(0, tt, rb):
            acc = jnp.zeros((rb, cb), F32)
            for j in range(CONV_K):
                ph, base = (first + j) % 8, (first + j) // 8 * 8 + r0
                if ph == 0:
                    x = ext_s[base:base + rb, c0:c0 + cb]
                else:
                    x = sh_s[ph - 1, base:base + rb, c0:c0 + cb]
                acc = acc + x * w_ref[0, j:j + 1, c0:c0 + cb]
            d_s[r0:r0 + rb, c0:c0 + cb] = acc
    d = d_s[...] + b_ref[0]
    mu = jnp.mean(d, axis=-1, keepdims=True)
    xc = d - mu
    y = xc * lax.rsqrt(jnp.mean(xc * xc, axis=-1, keepdims=True) + EPS) * g_ref[0] + beta_ref[0]
    o_ref[0] = (_silu(y) * sgc_ref[0].astype(F32)).astype(BF16)
    if tt >= CONV_HALO:
        ext_s[0:CONV_HALO] = ext_s[tt:tt + CONV_HALO]


def _conv(lidx, u, state, state_rows, conv_w, conv_b, ln_g, ln_b, sgc, tt):
    nb, t, cw = u.shape
    row = lambda b, i, l: (b, i, 0)
    lay = lambda b, i, l: (l[0], 0, 0)
    rb = min(32, tt)
    return pl.pallas_call(
        functools.partial(_conv_kernel, tt=tt, rb=rb, cb=min(256, cw)),
        grid_spec=pltpu.PrefetchScalarGridSpec(
            num_scalar_prefetch=1, grid=(nb, t // tt),
            in_specs=[pl.BlockSpec((1, tt, cw), row),
                      pl.BlockSpec((1, CONV_HALO, cw), lambda b, i, l: (state_rows(l[0], b), 0, 0)),
                      pl.BlockSpec((1, CONV_HALO, cw), lay),
                      pl.BlockSpec((1, 1, cw), lay), pl.BlockSpec((1, 1, cw), lay), pl.BlockSpec((1, 1, cw), lay),
                      pl.BlockSpec((1, tt, cw), row)],
            out_specs=pl.BlockSpec((1, tt, cw), row),
            scratch_shapes=[pltpu.VMEM((CONV_HALO + tt, cw), F32),
                            pltpu.VMEM((7, tt + CONV_HALO - 8, cw), F32),
                            pltpu.VMEM((tt, cw), F32)]),
        out_shape=jax.ShapeDtypeStruct((nb, t, cw), BF16),
        compiler_params=_params(("parallel", "arbitrary")),
        name="conformer_conv",
    )(lidx, u, state, conv_w, conv_b, ln_g, ln_b, sgc)


def _out_kernel(l_ref, a_ref, cv_ref, gm_ref, x_ref, gate_ref, wa_ref, wc_ref, wo_ref, o_ref):
    d = x_ref.shape[-1]
    y_att = jnp.dot(a_ref[0], wa_ref[0], preferred_element_type=F32)
    y_conv = jnp.dot(cv_ref[0], wc_ref[0], preferred_element_type=F32)
    merged = gm_ref[0, :, :d].astype(F32) * y_att + gm_ref[0, :, d:].astype(F32) * y_conv
    out = jnp.dot(merged.astype(BF16), wo_ref[0], preferred_element_type=F32)
    o_ref[0] = x_ref[0] + gate_ref[0] * out


def _out(lidx, a, cv, gm, x, mod, mod_rows, w_ao, w_co, w_o, tm):
    nb, t, d = x.shape
    cw = cv.shape[-1]
    row = lambda b, i, l: (b, i, 0)
    lay = lambda b, i, l: (l[0], 0, 0)
    return pl.pallas_call(
        _out_kernel,
        grid_spec=pltpu.PrefetchScalarGridSpec(
            num_scalar_prefetch=1, grid=(nb, t // tm),
            in_specs=[pl.BlockSpec((1, tm, ATT_W), row), pl.BlockSpec((1, tm, cw), row),
                      pl.BlockSpec((1, tm, 2 * d), row), pl.BlockSpec((1, tm, d), row),
                      pl.BlockSpec((1, mod.shape[1], d), lambda b, i, l: (mod_rows(l[0], b), 0, 2)),
                      _resident((1, ATT_W, d), lay), _resident((1, cw, d), lay), _resident((1, d, d), lay)],
            out_specs=pl.BlockSpec((1, tm, d), row)),
        out_shape=jax.ShapeDtypeStruct((nb, t, d), F32),
        input_output_aliases={4: 0},
        compiler_params=_params(("parallel", "parallel")),
        name="merge_out_proj",
    )(lidx, a, cv, gm, x, mod, w_ao, w_co, w_o)


def _rope_tables(pos, head_dim):
    rd = head_dim // 4
    half = rd // 2
    inv = ROPE_THETA ** (-jnp.arange(half, dtype=F32) / half)
    ang = pos.astype(F32)[:, None] * inv[None, :]
    cos, sin = jnp.cos(ang), jnp.sin(ang)
    n = pos.shape[0]
    pad = lambda w: jnp.zeros((n, w), F32)
    c = jnp.concatenate([cos, cos, jnp.ones((n, head_dim - rd), F32)], axis=-1)
    s1 = jnp.concatenate([-sin, pad(head_dim - half)], axis=-1)
    s2 = jnp.concatenate([pad(half), sin, pad(head_dim - rd)], axis=-1)
    rep = LANE // head_dim
    return tuple(jnp.tile(a, (1, rep)) for a in (c, s1, s2))


def _pad_rows(a, rows, axis):
    pad = [(0, 0)] * a.ndim
    pad[axis] = (0, rows - a.shape[axis])
    return jnp.pad(a, pad)


def kernel(x_prompt, x_sample, c_prompt, c_sample, cache_k, cache_v, cache_kidx, state_conv, page_table, norm_g, w_ada, b_ada, w_in, q_norm_g, k_norm_g, idx_k_norm_g, conv_w, conv_b, conv_ln_g, conv_ln_b, w_attn_out, w_conv_out, w_out):
    batch, seq, d = x_prompt.shape
    dec_batch, dec_seq, _ = x_sample.shape
    n_layers, n_phys, page = cache_k.shape[:3]
    n_pages = page_table.shape[1]
    past_len = n_pages * page
    cw = conv_w.shape[-1]
    topk_p = min(TOPK_MAX, seq // 4)
    topk_s = min(TOPK_MAX, (past_len + dec_seq) // 4)
    n_dec = dec_batch * dec_seq
    assert dec_seq <= SAMPLE_ROWS and CONV_K - 1 <= CONV_HALO

    o_ki = ATT_W + 2 * KV_W + QI_W
    o_ga = o_ki + IDX_DIM + N_IDX_HEADS
    o_glu = o_ga + ATT_W
    o_mg = o_glu + 3 * cw
    w_in16 = w_in.astype(BF16)
    w_a = w_in16[..., :o_ki]
    w_kw = _pad_rows(w_in16[..., o_ki:o_ga], LANE, 2)
    w_ga = w_in16[..., o_ga:o_glu]
    w_b = w_in16[..., o_glu:o_mg]
    w_mg = w_in16[..., o_mg:]
    w_ao, w_co, w_o = w_attn_out.astype(BF16), w_conv_out.astype(BF16), w_out.astype(BF16)
    qg = q_norm_g.reshape(n_layers, 1, HEAD_DIM)
    kg = k_norm_g.reshape(n_layers, 1, HEAD_DIM)
    gi = _pad_rows(idx_k_norm_g, LANE, 1).reshape(n_layers, 1, LANE)
    conv_w_p = _pad_rows(conv_w, CONV_HALO, 1)
    as_row = lambda a: a.reshape(n_layers, 1, a.shape[-1])

    n_c = batch + dec_batch
    c_all = _pad_rows(jnp.concatenate([c_prompt, c_sample], axis=0), -(-n_c // 8) * 8, 0)
    mod = _modulation(c_all, w_ada, b_ada)
    mod_p = mod[:, :batch].reshape(n_layers * batch, 1, 3 * d)
    mod_s = jnp.repeat(mod[:, batch:n_c], dec_seq, axis=1)
    mod_rows_p = lambda l, b: l * batch + b
    mod_rows_s = lambda l, b: l

    rope_p = _rope_tables(jnp.arange(seq, dtype=jnp.int32), HEAD_DIM)
    idx_p = _rope_tables(jnp.arange(seq, dtype=jnp.int32), IDX_DIM)
    pos_s = jnp.tile(past_len + jnp.arange(dec_seq, dtype=jnp.int32), dec_batch)
    rope_s = _rope_tables(pos_s, HEAD_DIM)
    idx_s = _rope_tables(pos_s, IDX_DIM)

    cache_k2 = cache_k.reshape(n_layers * n_phys, page * N_KV_HEADS, HEAD_DIM)
    cache_v2 = cache_v.reshape(n_layers * n_phys, page * N_KV_HEADS, HEAD_DIM)
    cache_ki2 = cache_kidx.reshape(n_layers * n_phys, page, IDX_DIM)
    state_p = jnp.zeros((1, CONV_HALO, cw), F32)
    state_s = jnp.pad(state_conv, ((0, 0), (0, 0), (CONV_HALO - (CONV_K - 1), 0), (0, 0)))
    state_s = state_s.reshape(n_layers * dec_batch, CONV_HALO, cw)

    tm = min(512, seq)
    tq = min(256, seq)
    ck = tm
    tt = min(256, seq)
    pages = min(16, n_pages)

    def to_sample_rows(a, axis):
        shp = a.shape[:axis] + (dec_batch, dec_seq) + a.shape[axis + 1:]
        a = jnp.moveaxis(a.reshape(shp), axis, 0)
        return _pad_rows(a, SAMPLE_ROWS, axis + 1)

    def layer(carry, l):
        xp, xs = carry
        lidx = jnp.reshape(l, (1,)).astype(jnp.int32)

        h = _norm(lidx, xp, as_row(norm_g), mod_p, mod_rows_p, tm)
        q, k32, kbf, v32, vt, qi, ki32, kibf, wi = _attn_in(lidx, h, w_a, w_kw, qg, kg, gi, rope_p, idx_p, tm, True)
        u, sgc = _conv_in(lidx, h, w_b, tm)
        sga, gm = _gate_in(lidx, h, w_ga, w_mg, tm)
        a = _prompt_attention(q, qi, wi, kbf, vt, kibf, sga, topk_p, tq, ck)
        cv = _conv(lidx, u, state_p, lambda l_, b: 0, conv_w_p, as_row(conv_b), as_row(conv_ln_g),
                   as_row(conv_ln_b), sgc, tt)
        xp_new = _out(lidx, a, cv, gm, xp, mod_p, mod_rows_p, w_ao, w_co, w_o, tm)
        outs_p = (k32.reshape(batch, seq, N_KV_HEADS, HEAD_DIM), v32.reshape(batch, seq, N_KV_HEADS, HEAD_DIM),
                  ki32, u[:, seq - (CONV_K - 1):])

        xs2 = xs.reshape(1, n_dec, d)
        h = _norm(lidx, xs2, as_row(norm_g), mod_s, mod_rows_s, n_dec)
        q, k32, kbf, v32, vbf, qi, ki32, kibf, wi = _attn_in(lidx, h, w_a, w_kw, qg, kg, gi, rope_s, idx_s, n_dec,
                                                             False)
        u, sgc = _conv_in(lidx, h, w_b, n_dec)
        sga, gm = _gate_in(lidx, h, w_ga, w_mg, n_dec)
        q_r = to_sample_rows(q[0], 1)
        qi_r = to_sample_rows(qi[0], 1)
        wi_r = to_sample_rows(wi[0], 0)
        sga_r = to_sample_rows(sga[0], 0)
        new_slots = lambda a_: _pad_rows(a_[0].reshape(dec_batch, dec_seq, a_.shape[-1]), LANE, 1)
        scores = _sample_scores(lidx, page_table, qi_r, wi_r, cache_ki2, n_phys, pages)
        thr, newsc, jcut = _sample_threshold(scores, qi_r, wi_r, new_slots(kibf), topk_s, dec_seq)
        a = _sample_attention(lidx, page_table, q_r, scores, thr, newsc, jcut, new_slots(kbf), new_slots(vbf),
                              sga_r, cache_k2, cache_v2, n_phys, pages)
        a = a[:, :dec_seq].reshape(1, n_dec, ATT_W)
        u_r = u[0].reshape(dec_batch, dec_seq, cw)
        cv = _conv(lidx, _pad_rows(u_r, SAMPLE_ROWS, 1), state_s, lambda l_, b: l_ * dec_batch + b, conv_w_p,
                   as_row(conv_b), as_row(conv_ln_g), as_row(conv_ln_b),
                   _pad_rows(sgc[0].reshape(dec_batch, dec_seq, cw), SAMPLE_ROWS, 1), SAMPLE_ROWS)
        cv = cv[:, :dec_seq].reshape(1, n_dec, cw)
        xs_new = _out(lidx, a, cv, gm, xs2, mod_s, mod_rows_s, w_ao, w_co, w_o, n_dec).reshape(xs.shape)
        past_state = lax.dynamic_index_in_dim(state_conv, l, 0, keepdims=False)
        new_conv_s = jnp.concatenate([past_state, u_r], axis=1)[:, -(CONV_K - 1):]
        outs_s = (k32.reshape(dec_batch, dec_seq, N_KV_HEADS, HEAD_DIM),
                  v32.reshape(dec_batch, dec_seq, N_KV_HEADS, HEAD_DIM),
                  ki32.reshape(dec_batch, dec_seq, IDX_DIM), new_conv_s)
        return (xp_new, xs_new), outs_p + outs_s

    (yp, ys), per_layer = lax.scan(layer, (x_prompt, x_sample), jnp.arange(n_layers, dtype=jnp.int32))
    return (yp, ys) + tuple(per_layer)
```

```python
import functools

import jax
import jax.numpy as jnp
from jax import lax
from jax.experimental import pallas as pl
from jax.experimental.pallas import tpu as pltpu

F32 = jnp.float32
BF16 = jnp.bfloat16
I32 = jnp.int32

N_HEADS = 8
HEAD_DIM = 128
N_KV_HEADS = 2
HEADS_PER_KV = N_HEADS // N_KV_HEADS
N_IDX_HEADS = 16
IDX_DIM = 64
TOPK_MAX = 256
CONV_K = 31
ROPE_THETA = 500000.0
EPS = 1e-6

LANE = 128
ATT_W = N_HEADS * HEAD_DIM
KV_W = N_KV_HEADS * HEAD_DIM
QI_W = N_IDX_HEADS * IDX_DIM
CONV_HALO = 32
SAMPLE_ROWS = 16
KEY_NEG_INF = -2139095041
KEY_POS_INF = 0x7F800000
VALUE_PASSES = 24
PASSES_PER_TRIP = 4
MAX_PASSES = VALUE_PASSES + 32
NEG_BIG = -1e30
LOG2E = 1.4426950408889634
VMEM_LIMIT = 56 * 1024 * 1024

_NT = (((1,), (1,)), ((), ()))


def _params(sem):
    return pltpu.CompilerParams(dimension_semantics=sem, vmem_limit_bytes=VMEM_LIMIT)


def _resident(block, imap):
    return pl.BlockSpec(block, imap, pipeline_mode=pl.Buffered(1))


def _silu(x):
    return x * jax.nn.sigmoid(x)


def _mod_kernel(c_ref, w_ref, b_ref, o_ref):
    a = _silu(c_ref[...]).astype(BF16)
    o_ref[0] = jnp.dot(a, w_ref[0].astype(BF16), preferred_element_type=F32) + b_ref[0]


def _modulation(c_all, w_ada, b_ada):
    n_layers, d, n3 = w_ada.shape
    rows = c_all.shape[0]
    tn = next(w for w in (512, 256, LANE, n3) if n3 % w == 0)
    return pl.pallas_call(
        _mod_kernel,
        grid=(n_layers, n3 // tn),
        in_specs=[pl.BlockSpec((rows, d), lambda l, j: (0, 0)),
                  pl.BlockSpec((1, d, tn), lambda l, j: (l, 0, j)),
                  pl.BlockSpec((1, 1, tn), lambda l, j: (l, 0, j))],
        out_specs=pl.BlockSpec((1, rows, tn), lambda l, j: (l, 0, j)),
        out_shape=jax.ShapeDtypeStruct((n_layers, rows, n3), F32),
        compiler_params=_params(("parallel", "parallel")),
        name="adaln_mod",
    )(c_all, w_ada, b_ada.reshape(n_layers, 1, n3))


def _norm_kernel(l_ref, x_ref, g_ref, sc_ref, sh_ref, o_ref):
    x = x_ref[0]
    y = x * lax.rsqrt(jnp.mean(x * x, axis=-1, keepdims=True) + EPS) * g_ref[0]
    o_ref[0] = (y * (1.0 + sc_ref[0]) + sh_ref[0]).astype(BF16)


def _norm(lidx, x, norm_g, mod, mod_rows, tm):
    nb, t, d = x.shape
    mod_imap = lambda part: (lambda b, i, l: (mod_rows(l[0], b), 0, part))
    return pl.pallas_call(
        _norm_kernel,
        grid_spec=pltpu.PrefetchScalarGridSpec(
            num_scalar_prefetch=1, grid=(nb, t // tm),
            in_specs=[pl.BlockSpec((1, tm, d), lambda b, i, l: (b, i, 0)),
                      pl.BlockSpec((1, 1, d), lambda b, i, l: (l[0], 0, 0)),
                      pl.BlockSpec((1, mod.shape[1], d), mod_imap(1)),
                      pl.BlockSpec((1, mod.shape[1], d), mod_imap(0))],
            out_specs=pl.BlockSpec((1, tm, d), lambda b, i, l: (b, i, 0))),
        out_shape=jax.ShapeDtypeStruct((nb, t, d), BF16),
        compiler_params=_params(("parallel", "parallel")),
        name="adaln_rmsnorm",
    )(lidx, x, norm_g, mod, mod)


def _rope(n, c, s1, s2, half):
    return n * c + pltpu.roll(n, LANE - half, 1) * s1 + pltpu.roll(n, half, 1) * s2


def _attn_in_kernel(l_ref, h_ref, w_ref, wkw_ref, qg_ref, kg_ref, gi_ref, rc_ref, rs1_ref, rs2_ref,
                    ic_ref, is1_ref, is2_ref,
                    q_ref, k32_ref, kbf_ref, v32_ref, vbf_ref, qi_ref, ki32_ref, kibf_ref, wi_ref, *, v_transposed):
    h = h_ref[0]
    rc, rs1, rs2 = rc_ref[...], rs1_ref[...], rs2_ref[...]
    ic, is1, is2 = ic_ref[...], is1_ref[...], is2_ref[...]

    def proj(c0, width):
        return jnp.dot(h, w_ref[0, :, c0:c0 + width], preferred_element_type=F32)

    def head_norm(z, g):
        return z * lax.rsqrt(jnp.mean(z * z, axis=-1, keepdims=True) + EPS) * g

    scale = HEAD_DIM ** -0.5 * LOG2E
    for pair in range(N_HEADS // 2):
        z2 = proj(pair * 2 * HEAD_DIM, 2 * HEAD_DIM)
        for j in range(2):
            z = z2[:, j * HEAD_DIM:(j + 1) * HEAD_DIM]
            r = _rope(head_norm(z, qg_ref[0]), rc, rs1, rs2, HEAD_DIM // 8)
            q_ref[0, 2 * pair + j] = (r * scale).astype(BF16)
    z2 = proj(ATT_W, KV_W)
    for g in range(N_KV_HEADS):
        z = z2[:, g * HEAD_DIM:(g + 1) * HEAD_DIM]
        r = _rope(head_norm(z, kg_ref[0]), rc, rs1, rs2, HEAD_DIM // 8)
        k32_ref[0, :, g * HEAD_DIM:(g + 1) * HEAD_DIM] = r
        kbf_ref[0, :, g * HEAD_DIM:(g + 1) * HEAD_DIM] = r.astype(BF16)
    z2 = proj(ATT_W + KV_W, KV_W)
    v32_ref[0] = z2
    if v_transposed:
        vbf_ref[0, 0] = z2.T.astype(BF16)
    else:
        vbf_ref[0] = z2.astype(BF16)

    lane = lax.broadcasted_iota(I32, (1, LANE), 1)
    lo = (lane < IDX_DIM).astype(F32)
    o_qi = ATT_W + 2 * KV_W
    for quad in range(N_IDX_HEADS // 4):
        z4 = proj(o_qi + quad * 2 * LANE, 2 * LANE)
        for j in range(2):
            r = _rope(z4[:, j * LANE:(j + 1) * LANE], ic, is1, is2, IDX_DIM // 8)
            qi_ref[0, 4 * quad + 2 * j] = (r * lo).astype(BF16)
            qi_ref[0, 4 * quad + 2 * j + 1] = (pltpu.roll(r, IDX_DIM, 1) * lo).astype(BF16)
    z = jnp.dot(h, wkw_ref[0], preferred_element_type=F32)
    ss = jnp.sum(z * z * lo, axis=-1, keepdims=True) * (1.0 / IDX_DIM)
    n = z * lax.rsqrt(ss + EPS) * gi_ref[0]
    r = _rope(n, ic, is1, is2, IDX_DIM // 8)
    ki32_ref[0] = r[:, :IDX_DIM]
    kibf_ref[0] = r.astype(BF16)
    wi_ref[0] = pltpu.roll(z, LANE - IDX_DIM, 1) * (N_IDX_HEADS ** -0.5 * IDX_DIM ** -0.5)


def _attn_in(lidx, h, w_a, w_kw, qg, kg, gi, rope_tabs, idx_tabs, tm, v_transposed):
    nb, t, d = h.shape
    na = w_a.shape[-1]
    row = lambda b, i, l: (b, i, 0)
    hm = lambda b, i, l: (b, 0, i, 0)
    lay = lambda b, i, l: (l[0], 0, 0)
    tab = pl.BlockSpec((tm, LANE), lambda b, i, l: (i, 0))
    if v_transposed:
        v_out = (jax.ShapeDtypeStruct((nb, t // tm, KV_W, tm), BF16),
                 pl.BlockSpec((1, 1, KV_W, tm), lambda b, i, l: (b, i, 0, 0)))
    else:
        v_out = (jax.ShapeDtypeStruct((nb, t, KV_W), BF16), pl.BlockSpec((1, tm, KV_W), row))
    outs = [
        (jax.ShapeDtypeStruct((nb, N_HEADS, t, HEAD_DIM), BF16), pl.BlockSpec((1, N_HEADS, tm, HEAD_DIM), hm)),
        (jax.ShapeDtypeStruct((nb, t, KV_W), F32), pl.BlockSpec((1, tm, KV_W), row)),
        (jax.ShapeDtypeStruct((nb, t, KV_W), BF16), pl.BlockSpec((1, tm, KV_W), row)),
        (jax.ShapeDtypeStruct((nb, t, KV_W), F32), pl.BlockSpec((1, tm, KV_W), row)),
        v_out,
        (jax.ShapeDtypeStruct((nb, N_IDX_HEADS, t, LANE), BF16), pl.BlockSpec((1, N_IDX_HEADS, tm, LANE), hm)),
        (jax.ShapeDtypeStruct((nb, t, IDX_DIM), F32), pl.BlockSpec((1, tm, IDX_DIM), row)),
        (jax.ShapeDtypeStruct((nb, t, LANE), BF16), pl.BlockSpec((1, tm, LANE), row)),
        (jax.ShapeDtypeStruct((nb, t, LANE), F32), pl.BlockSpec((1, tm, LANE), row)),
    ]
    return pl.pallas_call(
        functools.partial(_attn_in_kernel, v_transposed=v_transposed),
        grid_spec=pltpu.PrefetchScalarGridSpec(
            num_scalar_prefetch=1, grid=(nb, t // tm),
            in_specs=[pl.BlockSpec((1, tm, d), row),
                      _resident((1, d, na), lay), _resident((1, d, LANE), lay),
                      pl.BlockSpec((1, 1, LANE), lay), pl.BlockSpec((1, 1, LANE), lay),
                      pl.BlockSpec((1, 1, LANE), lay),
                      tab, tab, tab, tab, tab, tab],
            out_specs=[o[1] for o in outs]),
        out_shape=[o[0] for o in outs],
        compiler_params=_params(("parallel", "parallel")),
        name="attn_in_proj",
    )(lidx, h, w_a, w_kw, qg, kg, gi, *rope_tabs, *idx_tabs)


def _conv_in_kernel(l_ref, h_ref, w_ref, u_ref, sgc_ref, *, cw, bw):
    h = h_ref[0]
    for c0 in range(0, cw, bw):
        za = jnp.dot(h, w_ref[0, :, c0:c0 + bw], preferred_element_type=F32)
        zb = jnp.dot(h, w_ref[0, :, cw + c0:cw + c0 + bw], preferred_element_type=F32)
        u_ref[0, :, c0:c0 + bw] = za * jax.nn.sigmoid(zb)
        zg = jnp.dot(h, w_ref[0, :, 2 * cw + c0:2 * cw + c0 + bw], preferred_element_type=F32)
        sgc_ref[0, :, c0:c0 + bw] = _silu(zg).astype(BF16)


def _conv_in(lidx, h, w_b, tm):
    nb, t, d = h.shape
    cw = w_b.shape[-1] // 3
    row = lambda b, i, l: (b, i, 0)
    return pl.pallas_call(
        functools.partial(_conv_in_kernel, cw=cw, bw=min(256, cw)),
        grid_spec=pltpu.PrefetchScalarGridSpec(
            num_scalar_prefetch=1, grid=(nb, t // tm),
            in_specs=[pl.BlockSpec((1, tm, d), row),
                      _resident((1, d, 3 * cw), lambda b, i, l: (l[0], 0, 0))],
            out_specs=[pl.BlockSpec((1, tm, cw), row), pl.BlockSpec((1, tm, cw), row)]),
        out_shape=[jax.ShapeDtypeStruct((nb, t, cw), F32), jax.ShapeDtypeStruct((nb, t, cw), BF16)],
        compiler_params=_params(("parallel", "parallel")),
        name="conv_in_proj",
    )(lidx, h, w_b)


def _gate_in_kernel(l_ref, h_ref, wga_ref, wmg_ref, sga_ref, gm_ref, *, bw):
    h = h_ref[0]
    for c0 in range(0, ATT_W, bw):
        z = jnp.dot(h, wga_ref[0, :, c0:c0 + bw], preferred_element_type=F32)
        sga_ref[0, :, c0:c0 + bw] = _silu(z).astype(BF16)
    for c0 in range(0, gm_ref.shape[-1], bw):
        z = jnp.dot(h, wmg_ref[0, :, c0:c0 + bw], preferred_element_type=F32)
        gm_ref[0, :, c0:c0 + bw] = jax.nn.sigmoid(z).astype(BF16)


def _gate_in(lidx, h, w_ga, w_mg, tm):
    nb, t, d = h.shape
    nm = w_mg.shape[-1]
    row = lambda b, i, l: (b, i, 0)
    lay = lambda b, i, l: (l[0], 0, 0)
    return pl.pallas_call(
        functools.partial(_gate_in_kernel, bw=256),
        grid_spec=pltpu.PrefetchScalarGridSpec(
            num_scalar_prefetch=1, grid=(nb, t // tm),
            in_specs=[pl.BlockSpec((1, tm, d), row), _resident((1, d, ATT_W), lay), _resident((1, d, nm), lay)],
            out_specs=[pl.BlockSpec((1, tm, ATT_W), row), pl.BlockSpec((1, tm, nm), row)]),
        out_shape=[jax.ShapeDtypeStruct((nb, t, ATT_W), BF16), jax.ShapeDtypeStruct((nb, t, nm), BF16)],
        compiler_params=_params(("parallel", "parallel")),
        name="gate_in_proj",
    )(lidx, h, w_ga, w_mg)


def _key_to_float(key):
    b = key ^ ((key >> 31) & jnp.int32(0x7FFFFFFF))
    return lax.bitcast_convert_type(b, F32)


def _float_to_key(x):
    b = lax.bitcast_convert_type(x, I32)
    return b ^ ((b >> 31) & jnp.int32(0x7FFFFFFF))


def _kth_largest(count_ge, lo_f, hi_f, n_ge_lo, k):
    kf = jnp.float32(k)
    select_all = n_ge_lo <= kf
    lo0 = jnp.where(select_all, jnp.int32(KEY_NEG_INF + 1), _float_to_key(lo_f))
    hi0 = _float_to_key(hi_f) + 1

    def one_pass(it, lo, hi, clo, active):
        lo_v = _key_to_float(lo)
        hi_v = _key_to_float(jnp.minimum(hi, jnp.int32(KEY_POS_INF)))
        mid_value = _float_to_key(0.5 * lo_v + 0.5 * hi_v)
        mid_image = (lo & hi) + ((lo ^ hi) >> 1)
        cand = jnp.where(it < VALUE_PASSES, mid_value, mid_image)
        cand = jnp.minimum(jnp.maximum(cand, lo + 1), hi - 1)
        cnt = count_ge(_key_to_float(cand))
        up = (cnt >= kf) & (active > 0.0)
        down = (cnt < kf) & (active > 0.0)
        lo = jnp.where(up, cand, lo)
        clo = jnp.where(up, cnt, clo)
        hi = jnp.where(down, cand, hi)
        active = jnp.where((clo == kf) | (hi <= lo + 1), 0.0, active)
        return lo, hi, clo, active

    def cond(st):
        return (st[0] < MAX_PASSES) & (jnp.sum(st[4]) > 0.0)

    def body(st):
        it, lo, hi, clo, active = st
        for sub in range(PASSES_PER_TRIP):
            lo, hi, clo, active = one_pass(it + sub, lo, hi, clo, active)
        return it + PASSES_PER_TRIP, lo, hi, clo, active

    active0 = jnp.where(select_all, 0.0, 1.0)
    st = lax.while_loop(cond, body, (jnp.int32(0), lo0, hi0, n_ge_lo, active0))
    thr = _key_to_float(jnp.maximum(st[1], jnp.int32(KEY_NEG_INF + 1)))
    return thr, st[3] - kf


def _tie_cutoff(tie_count, surplus, n_keys):
    keep = tie_count(jnp.full(surplus.shape, n_keys, I32)) - surplus

    def body(_, st):
        lo, hi = st
        mid = lo + ((hi - lo) >> 1)
        ok = tie_count(mid) >= keep
        return jnp.where(ok, lo, mid), jnp.where(ok, mid, hi)

    lo0 = jnp.full(surplus.shape, -1, I32)
    hi0 = jnp.full(surplus.shape, n_keys - 1, I32)
    _, hi = lax.fori_loop(0, max(n_keys - 1, 1).bit_length() + 1, body, (lo0, hi0))
    return jnp.where(surplus > 0.0, hi, jnp.int32(n_keys))


def _fold_rows(x, op):
    r, n = x.shape
    if r > 64:
        x = op(x.reshape(r // 64, 64, n), axis=0)
    return op(x.reshape(x.shape[0] // 8, 8, n), axis=0)


def _index_scores(qi, wi, kc, rows):
    sc = jnp.zeros((rows, kc.shape[0]), F32)
    for hg in range(N_IDX_HEADS // 4):
        d = lax.dot_general(qi[hg * 4 * rows:(hg + 1) * 4 * rows], kc, _NT, preferred_element_type=F32)
        for j in range(4):
            hd = hg * 4 + j
            sc = sc + jnp.maximum(d[j * rows:(j + 1) * rows], 0.0) * wi[:, hd:hd + 1]
    return sc


def _softmax_step(s, vc, mask, m_ref, l_ref, acc_ref, rows):
    ck = vc.shape[0]
    s = jnp.where(mask[None], s.reshape(HEADS_PER_KV, rows, ck), NEG_BIG).reshape(HEADS_PER_KV * rows, ck)
    m_prev = m_ref[:, :1]
    m_new = jnp.maximum(m_prev, jnp.max(s, axis=-1, keepdims=True))
    alpha = jnp.exp2(m_prev - m_new)
    p = jnp.exp2(s - m_new)
    l_ref[...] = jnp.broadcast_to(alpha * l_ref[:, :1] + jnp.sum(p, axis=-1, keepdims=True), l_ref.shape)
    acc_ref[...] = alpha * acc_ref[...] + jnp.dot(p.astype(BF16), vc, preferred_element_type=F32)
    m_ref[...] = jnp.broadcast_to(m_new, m_ref.shape)


def _softmax_init(m_ref, l_ref, acc_ref):
    m_ref[...] = jnp.full(m_ref.shape, NEG_BIG, F32)
    l_ref[...] = jnp.zeros(l_ref.shape, F32)
    acc_ref[...] = jnp.zeros(acc_ref.shape, F32)


def _softmax_finish(g, sga_ref, o_ref, l_ref, acc_ref, rows):
    o = acc_ref[...] / l_ref[:, :1]
    for j in range(HEADS_PER_KV):
        c0 = (g * HEADS_PER_KV + j) * HEAD_DIM
        gate = sga_ref[0, :, c0:c0 + HEAD_DIM].astype(F32)
        o_ref[0, :, c0:c0 + HEAD_DIM] = (o[j * rows:(j + 1) * rows] * gate).astype(BF16)


def _prompt_attn_kernel(q_ref, qi_ref, wi_ref, k_ref, vt_ref, ki_ref, sga_ref, o_ref,
                        sc_s, m_s, l_s, acc_s, *, tq, ck, topk):
    row0 = pl.program_id(1) * tq
    nch = (row0 + tq + ck - 1) // ck
    wit = wi_ref[0].T
    kpos = lax.broadcasted_iota(I32, (ck, tq), 0)
    qpos = row0 + lax.broadcasted_iota(I32, (ck, tq), 1)

    def score_chunk(c, carry):
        kc = ki_ref[0, pl.ds(pl.multiple_of(c * ck, ck), ck), :]
        sc = jnp.zeros((ck, tq), F32)
        for hd in range(N_IDX_HEADS):
            d = lax.dot_general(kc, qi_ref[0, hd], _NT, preferred_element_type=F32)
            sc = sc + jnp.maximum(d, 0.0) * wit[hd:hd + 1, :]
        sc_s[c] = jnp.where(kpos + c * ck <= qpos, sc, -jnp.inf)
        return carry

    lax.fori_loop(0, nch, score_chunk, 0)

    def count_ge(thr):
        def body(c, acc):
            return acc + _fold_rows(jnp.where(sc_s[c] >= thr, 1.0, 0.0), jnp.sum)

        acc = lax.fori_loop(0, nch, body, jnp.zeros((8, tq), F32))
        return jnp.sum(acc, axis=0, keepdims=True)

    def fold_minmax(s, carry, masked):
        mx, mn = carry
        mx = jnp.maximum(mx, _fold_rows(s, jnp.max))
        if masked:
            s = jnp.where(s == -jnp.inf, jnp.inf, s)
        return mx, jnp.minimum(mn, _fold_rows(s, jnp.min))

    init = (jnp.full((8, tq), -jnp.inf, F32), jnp.full((8, tq), jnp.inf, F32))
    mm = lax.fori_loop(0, nch - 1, lambda c, carry: fold_minmax(sc_s[c], carry, False), init)
    mx, mn = fold_minmax(sc_s[nch - 1], mm, True)
    n_visible = (qpos[:1] + 1).astype(F32)
    thr, surplus = _kth_largest(count_ge, jnp.min(mn, axis=0, keepdims=True), jnp.max(mx, axis=0, keepdims=True),
                                n_visible, topk)

    @pl.when(jnp.max(surplus) > 0.0)
    def _():
        def tie_count(jcut):
            def body(c, acc):
                hit = (sc_s[c] == thr) & (kpos + c * ck <= jcut)
                return acc + _fold_rows(jnp.where(hit, 1.0, 0.0), jnp.sum)

            acc = lax.fori_loop(0, nch, body, jnp.zeros((8, tq), F32))
            return jnp.sum(acc, axis=0, keepdims=True)

        jcut = _tie_cutoff(tie_count, surplus, sc_s.shape[0] * ck)

        def hide(c, carry):
            s = sc_s[c]
            sc_s[c] = jnp.where((s == thr) & (kpos + c * ck > jcut), -jnp.inf, s)
            return carry

        lax.fori_loop(0, nch, hide, 0)

    _softmax_init(m_s, l_s, acc_s)

    def attend_chunk(c, carry):
        bias = jnp.where(sc_s[c] >= thr, 0.0, NEG_BIG)
        off = pl.multiple_of(c * ck, ck)

        def qk(g):
            kc = k_ref[0, pl.ds(off, ck), g * HEAD_DIM:(g + 1) * HEAD_DIM]
            qg = q_ref[0, g * HEADS_PER_KV:(g + 1) * HEADS_PER_KV].reshape(HEADS_PER_KV * tq, HEAD_DIM)
            return lax.dot_general(kc, qg, _NT, preferred_element_type=F32)

        logits = [qk(g) for g in range(N_KV_HEADS)]
        for g in range(N_KV_HEADS):
            ps, alphas = [], []
            for j in range(HEADS_PER_KV):
                hd = g * HEADS_PER_KV + j
                s = logits[g][:, j * tq:(j + 1) * tq] + bias
                m_prev = m_s[hd]
                m_new = jnp.maximum(m_prev, jnp.max(_fold_rows(s, jnp.max), axis=0, keepdims=True))
                alpha = jnp.exp2(m_prev - m_new)
                p = jnp.exp2(s - m_new)
                l_s[hd] = alpha * l_s[hd] + jnp.sum(_fold_rows(p, jnp.sum), axis=0, keepdims=True)
                m_s[hd] = m_new
                ps.append(p.astype(BF16))
                alphas.append(alpha)
            vtc = vt_ref[0, c, g * HEAD_DIM:(g + 1) * HEAD_DIM, :]
            pv = jnp.dot(vtc, jnp.concatenate(ps, axis=1), preferred_element_type=F32)
            acc_s[g] = jnp.concatenate(alphas, axis=1) * acc_s[g] + pv
        return carry

    lax.fori_loop(0, nch, attend_chunk, 0)
    for hd in range(N_HEADS):
        g, j = divmod(hd, HEADS_PER_KV)
        o = (acc_s[g, :, j * tq:(j + 1) * tq] / l_s[hd]).T
        gate = sga_ref[0, :, hd * HEAD_DIM:(hd + 1) * HEAD_DIM].astype(F32)
        o_ref[0, :, hd * HEAD_DIM:(hd + 1) * HEAD_DIM] = (o * gate).astype(BF16)


def _prompt_attention(q, qi, wi, kbf, vt, kibf, sga, topk, tq, ck):
    nb, _, t, _ = q.shape
    assert vt.shape == (nb, t // ck, KV_W, ck)
    full = lambda b, i: (b, 0, 0)
    row = lambda b, i: (b, i, 0)
    hm = lambda b, i: (b, 0, i, 0)
    return pl.pallas_call(
        functools.partial(_prompt_attn_kernel, tq=tq, ck=ck, topk=topk),
        grid=(nb, t // tq),
        in_specs=[pl.BlockSpec((1, N_HEADS, tq, HEAD_DIM), hm),
                  pl.BlockSpec((1, N_IDX_HEADS, tq, LANE), hm),
                  pl.BlockSpec((1, tq, LANE), row),
                  pl.BlockSpec((1, t, KV_W), full),
                  pl.BlockSpec((1, t // ck, KV_W, ck), lambda b, i: (b, 0, 0, 0)),
                  pl.BlockSpec((1, t, LANE), full),
                  pl.BlockSpec((1, tq, ATT_W), row)],
        out_specs=pl.BlockSpec((1, tq, ATT_W), row),
        out_shape=jax.ShapeDtypeStruct((nb, t, ATT_W), BF16),
        scratch_shapes=[pltpu.VMEM((t // ck, ck, tq), F32),
                        pltpu.VMEM((N_HEADS, 1, tq), F32),
                        pltpu.VMEM((N_HEADS, 1, tq), F32),
                        pltpu.VMEM((N_KV_HEADS, HEAD_DIM, HEADS_PER_KV * tq), F32)],
        compiler_params=_params(("parallel", "arbitrary")),
        name="prompt_sparse_attention",
    )(q, qi, wi, kbf, vt, kibf, sga)


def _paged_fetch(l_ref, pt_ref, hbm_refs, raw_refs, sems, *, pages, n_phys):
    b, j = pl.program_id(0), pl.program_id(1)
    nj = pl.num_programs(1)
    step = b * nj + j
    slot = step % 2

    def copies(bb, jj, sl):
        out = []
        for p in range(pages):
            row = l_ref[0] * n_phys + pt_ref[bb, jj * pages + p]
            for a, (hbm, raw) in enumerate(zip(hbm_refs, raw_refs)):
                n = hbm.shape[1]
                out.append(pltpu.make_async_copy(hbm.at[row], raw.at[sl, pl.ds(p * n, n)], sems.at[a, sl]))
        return out

    @pl.when(step == 0)
    def _():
        for cp in copies(b, j, slot):
            cp.start()

    @pl.when(step + 1 < pl.num_programs(0) * nj)
    def _():
        for cp in copies((step + 1) // nj, (step + 1) % nj, 1 - slot):
            cp.start()

    for cp in copies(b, j, slot):
        cp.wait()
    return slot


def _sample_scores_kernel(l_ref, pt_ref, qi_ref, wi_ref, cache_hbm, o_ref, raw, sems, *, pages, n_phys):
    slot = _paged_fetch(l_ref, pt_ref, [cache_hbm], [raw], sems, pages=pages, n_phys=n_phys)
    qi = qi_ref[0].reshape(N_IDX_HEADS * SAMPLE_ROWS, LANE)[:, :IDX_DIM]
    o_ref[0] = _index_scores(qi, wi_ref[0], raw[slot].astype(BF16), SAMPLE_ROWS)


def _sample_scores(lidx, page_table, qi, wi, cache_kidx, n_phys, pages):
    nb, n_pages = page_table.shape
    page = cache_kidx.shape[1]
    return pl.pallas_call(
        functools.partial(_sample_scores_kernel, pages=pages, n_phys=n_phys),
        grid_spec=pltpu.PrefetchScalarGridSpec(
            num_scalar_prefetch=2, grid=(nb, n_pages // pages),
            in_specs=[pl.BlockSpec((1, N_IDX_HEADS, SAMPLE_ROWS, LANE), lambda b, j, l, pt: (b, 0, 0, 0)),
                      pl.BlockSpec((1, SAMPLE_ROWS, LANE), lambda b, j, l, pt: (b, 0, 0)),
                      pl.BlockSpec(memory_space=pl.ANY)],
            out_specs=pl.BlockSpec((1, SAMPLE_ROWS, pages * page), lambda b, j, l, pt: (b, 0, j)),
            scratch_shapes=[pltpu.VMEM((2, pages * page, IDX_DIM), F32),
                            pltpu.SemaphoreType.DMA((1, 2))]),
        out_shape=jax.ShapeDtypeStruct((nb, SAMPLE_ROWS, n_pages * page), F32),
        compiler_params=_params(("arbitrary", "arbitrary")),
        name="sample_index_scores",
    )(lidx, page_table, qi, wi, cache_kidx)


def _sample_threshold_kernel(sc_ref, qi_ref, wi_ref, kin_ref, thr_ref, newsc_ref, jcut_ref, *, topk, n_new):
    rows = SAMPLE_ROWS
    n_past = sc_ref.shape[-1]
    qi = qi_ref[0].reshape(N_IDX_HEADS * rows, LANE)
    sc_new = _index_scores(qi, wi_ref[0], kin_ref[0], rows)
    r = lax.broadcasted_iota(I32, sc_new.shape, 0)
    c = lax.broadcasted_iota(I32, sc_new.shape, 1)
    sc_new = jnp.where((c <= r) & (c < n_new), sc_new, -jnp.inf)
    newsc_ref[0] = sc_new

    def count_rows(hit_new, hit_past):
        accs = [jnp.where(hit_new, 1.0, 0.0)] + [jnp.zeros((rows, LANE), F32)] * 7
        for j in range(n_past // LANE):
            accs[j % 8] = accs[j % 8] + jnp.where(hit_past(sc_ref[0, :, j * LANE:(j + 1) * LANE], j), 1.0, 0.0)
        acc = (accs[0] + accs[1]) + (accs[2] + accs[3]) + ((accs[4] + accs[5]) + (accs[6] + accs[7]))
        return jnp.sum(acc, axis=-1, keepdims=True)

    def count_ge(thr):
        tb = jnp.broadcast_to(thr, (rows, LANE))
        return count_rows(sc_new >= tb, lambda s, j: s >= tb)

    past = sc_ref[0]
    mx = jnp.maximum(jnp.max(past, axis=-1, keepdims=True), jnp.max(sc_new, axis=-1, keepdims=True))
    mn = jnp.minimum(jnp.min(past, axis=-1, keepdims=True),
                     jnp.min(jnp.where(sc_new == -jnp.inf, jnp.inf, sc_new), axis=-1, keepdims=True))
    n_visible = (n_past + jnp.minimum(r[:, :1] + 1, n_new)).astype(F32)
    thr, surplus = _kth_largest(count_ge, mn, mx, n_visible, topk)
    surplus = jnp.where(r[:, :1] < n_new, surplus, 0.0)
    thr_ref[0] = jnp.broadcast_to(thr, (rows, LANE))
    n_keys = n_past + LANE
    jcut_ref[0] = jnp.full((rows, LANE), n_keys, I32)

    @pl.when(jnp.max(surplus) > 0.0)
    def _():
        tb = jnp.broadcast_to(thr, (rows, LANE))

        def tie_count(jcut):
            jb = jnp.broadcast_to(jcut, (rows, LANE))
            return count_rows((sc_new == tb) & (n_past + c <= jb), lambda s, j: (s == tb) & (j * LANE + c <= jb))

        jcut_ref[0] = jnp.broadcast_to(_tie_cutoff(tie_count, surplus, n_keys), (rows, LANE))


def _sample_threshold(scores, qi, wi, ki_new, topk, n_new):
    nb, rows, n_keys = scores.shape
    b3 = lambda b: (b, 0, 0)
    return pl.pallas_call(
        functools.partial(_sample_threshold_kernel, topk=topk, n_new=n_new),
        grid=(nb,),
        in_specs=[pl.BlockSpec((1, rows, n_keys), b3),
                  pl.BlockSpec((1, N_IDX_HEADS, rows, LANE), lambda b: (b, 0, 0, 0)),
                  pl.BlockSpec((1, rows, LANE), b3),
                  pl.BlockSpec((1, LANE, LANE), b3)],
        out_specs=[pl.BlockSpec((1, rows, LANE), b3)] * 3,
        out_shape=[jax.ShapeDtypeStruct((nb, rows, LANE), F32), jax.ShapeDtypeStruct((nb, rows, LANE), F32),
                   jax.ShapeDtypeStruct((nb, rows, LANE), I32)],
        compiler_params=_params(("parallel",)),
        name="sample_topk_threshold",
    )(scores, qi, wi, ki_new)


def _sample_attn_kernel(l_ref, pt_ref, q_ref, sc_ref, thr_ref, newsc_ref, jcut_ref, knew_ref, vnew_ref, sga_ref,
                        ck_hbm, cv_hbm, o_ref, kraw, vraw, sems, m_s, l_s, acc_s, *, pages, n_phys):
    slot = _paged_fetch(l_ref, pt_ref, [ck_hbm, cv_hbm], [kraw, vraw], sems, pages=pages, n_phys=n_phys)
    rows = SAMPLE_ROWS
    n_chunk = kraw.shape[1] // N_KV_HEADS
    j = pl.program_id(1)
    thr = thr_ref[0][:, :1]
    jcut = jcut_ref[0][:, :1]

    @pl.when(j == 0)
    def _():
        for g in range(N_KV_HEADS):
            _softmax_init(m_s.at[g], l_s.at[g], acc_s.at[g])

    def q_group(g):
        return q_ref[0, g * HEADS_PER_KV:(g + 1) * HEADS_PER_KV].reshape(HEADS_PER_KV * rows, HEAD_DIM)

    def selected(s, first_index):
        idx = first_index + lax.broadcasted_iota(I32, s.shape, 1)
        return (s > thr) | ((s == thr) & (idx <= jcut))

    def head_rows(raw, g):
        return raw[slot, pl.ds(g, n_chunk, stride=N_KV_HEADS), :].astype(BF16)

    mask = selected(sc_ref[0], j * n_chunk)
    logits = [lax.dot_general(q_group(g), head_rows(kraw, g), _NT, preferred_element_type=F32)
              for g in range(N_KV_HEADS)]
    for g in range(N_KV_HEADS):
        _softmax_step(logits[g], head_rows(vraw, g), mask, m_s.at[g], l_s.at[g], acc_s.at[g], rows)

    @pl.when(j == pl.num_programs(1) - 1)
    def _():
        mask_new = selected(newsc_ref[0], pl.num_programs(1) * n_chunk)
        for g in range(N_KV_HEADS):
            kc = knew_ref[0, :, g * HEAD_DIM:(g + 1) * HEAD_DIM]
            vc = vnew_ref[0, :, g * HEAD_DIM:(g + 1) * HEAD_DIM]
            s = lax.dot_general(q_group(g), kc, _NT, preferred_element_type=F32)
            _softmax_step(s, vc, mask_new, m_s.at[g], l_s.at[g], acc_s.at[g], rows)
            _softmax_finish(g, sga_ref, o_ref, l_s.at[g], acc_s.at[g], rows)


def _sample_attention(lidx, page_table, q, scores, thr, newsc, jcut, k_new, v_new, sga, cache_k, cache_v, n_phys,
                      pages):
    nb, n_pages = page_table.shape
    page = cache_k.shape[1] // N_KV_HEADS
    rows = SAMPLE_ROWS
    b3 = lambda b, j, l, pt: (b, 0, 0)
    return pl.pallas_call(
        functools.partial(_sample_attn_kernel, pages=pages, n_phys=n_phys),
        grid_spec=pltpu.PrefetchScalarGridSpec(
            num_scalar_prefetch=2, grid=(nb, n_pages // pages),
            in_specs=[pl.BlockSpec((1, N_HEADS, rows, HEAD_DIM), lambda b, j, l, pt: (b, 0, 0, 0)),
                      pl.BlockSpec((1, rows, pages * page), lambda b, j, l, pt: (b, 0, j)),
                      pl.BlockSpec((1, rows, LANE), b3), pl.BlockSpec((1, rows, LANE), b3),
                      pl.BlockSpec((1, rows, LANE), b3),
                      pl.BlockSpec((1, LANE, KV_W), b3), pl.BlockSpec((1, LANE, KV_W), b3),
                      pl.BlockSpec((1, rows, ATT_W), b3),
                      pl.BlockSpec(memory_space=pl.ANY), pl.BlockSpec(memory_space=pl.ANY)],
            out_specs=pl.BlockSpec((1, rows, ATT_W), b3),
            scratch_shapes=[pltpu.VMEM((2, pages * page * N_KV_HEADS, HEAD_DIM), F32),
                            pltpu.VMEM((2, pages * page * N_KV_HEADS, HEAD_DIM), F32),
                            pltpu.SemaphoreType.DMA((2, 2)),
                            pltpu.VMEM((N_KV_HEADS, HEADS_PER_KV * rows, LANE), F32),
                            pltpu.VMEM((N_KV_HEADS, HEADS_PER_KV * rows, LANE), F32),
                            pltpu.VMEM((N_KV_HEADS, HEADS_PER_KV * rows, HEAD_DIM), F32)]),
        out_shape=jax.ShapeDtypeStruct((nb, rows, ATT_W), BF16),
        compiler_params=_params(("arbitrary", "arbitrary")),
        name="sample_sparse_attention",
    )(lidx, page_table, q, scores, thr, newsc, jcut, k_new, v_new, sga, cache_k, cache_v)


def _conv_kernel(l_ref, u_ref, st_ref, w_ref, b_ref, g_ref, beta_ref, sgc_ref, o_ref, ext_s, sh_s, d_s,
                 *, tt, rb, cb):
    cw = u_ref.shape[-1]

    @pl.when(pl.program_id(1) == 0)
    def _():
        ext_s[0:CONV_HALO] = st_ref[0]

    ext_s[CONV_HALO:CONV_HALO + tt] = u_ref[0]
    first = CONV_HALO - (CONV_K - 1)
    n_sh = sh_s.shape[1]
    for ph in range(1, 8):
        sh_s[ph - 1] = ext_s[ph:ph + n_sh, :]
    for c0 in range(0, cw, cb):
        for r0 in range(0, tt, rb):
            acc = jnp.zeros((rb, cb), F32)
            for j in range(CONV_K):
                ph, base = (first + j) % 8, (first + j) // 8 * 8 + r0
                if ph == 0:
                    x = ext_s[base:base + rb, c0:c0 + cb]
                else:
                    x = sh_s[ph - 1, base:base + rb, c0:c0 + cb]
                acc = acc + x * w_ref[0, j:j + 1, c0:c0 + cb]
            d_s[r0:r0 + rb, c0:c0 + cb] = acc
    d = d_s[...] + b_ref[0]
    mu = jnp.mean(d, axis=-1, keepdims=True)
    xc = d - mu
    y = xc * lax.rsqrt(jnp.mean(xc * xc, axis=-1, keepdims=True) + EPS) * g_ref[0] + beta_ref[0]
    o_ref[0] = (_silu(y) * sgc_ref[0].astype(F32)).astype(BF16)
    if tt >= CONV_HALO:
        ext_s[0:CONV_HALO] = ext_s[tt:tt + CONV_HALO]


def _conv(lidx, u, state, state_rows, conv_w, conv_b, ln_g, ln_b, sgc, tt):
    nb, t, cw = u.shape
    row = lambda b, i, l: (b, i, 0)
    lay = lambda b, i, l: (l[0], 0, 0)
    rb = min(32, tt)
    return pl.pallas_call(
        functools.partial(_conv_kernel, tt=tt, rb=rb, cb=min(256, cw)),
        grid_spec=pltpu.PrefetchScalarGridSpec(
            num_scalar_prefetch=1, grid=(nb, t // tt),
            in_specs=[pl.BlockSpec((1, tt, cw), row),
                      pl.BlockSpec((1, CONV_HALO, cw), lambda b, i, l: (state_rows(l[0], b), 0, 0)),
                      pl.BlockSpec((1, CONV_HALO, cw), lay),
                      pl.BlockSpec((1, 1, cw), lay), pl.BlockSpec((1, 1, cw), lay), pl.BlockSpec((1, 1, cw), lay),
                      pl.BlockSpec((1, tt, cw), row)],
            out_specs=pl.BlockSpec((1, tt, cw), row),
            scratch_shapes=[pltpu.VMEM((CONV_HALO + tt, cw), F32),
                            pltpu.VMEM((7, tt + CONV_HALO - 8, cw), F32),
                            pltpu.VMEM((tt, cw), F32)]),
        out_shape=jax.ShapeDtypeStruct((nb, t, cw), BF16),
        compiler_params=_params(("parallel", "arbitrary")),
        name="conformer_conv",
    )(lidx, u, state, conv_w, conv_b, ln_g, ln_b, sgc)


def _out_kernel(l_ref, a_ref, cv_ref, gm_ref, x_ref, gate_ref, wa_ref, wc_ref, wo_ref, o_ref):
    d = x_ref.shape[-1]
    y_att = jnp.dot(a_ref[0], wa_ref[0], preferred_element_type=F32)
    y_conv = jnp.dot(cv_ref[0], wc_ref[0], preferred_element_type=F32)
    merged = gm_ref[0, :, :d].astype(F32) * y_att + gm_ref[0, :, d:].astype(F32) * y_conv
    out = jnp.dot(merged.astype(BF16), wo_ref[0], preferred_element_type=F32)
    o_ref[0] = x_ref[0] + gate_ref[0] * out


def _out(lidx, a, cv, gm, x, mod, mod_rows, w_ao, w_co, w_o, tm):
    nb, t, d = x.shape
    cw = cv.shape[-1]
    row = lambda b, i, l: (b, i, 0)
    lay = lambda b, i, l: (l[0], 0, 0)
    return pl.pallas_call(
        _out_kernel,
        grid_spec=pltpu.PrefetchScalarGridSpec(
            num_scalar_prefetch=1, grid=(nb, t // tm),
            in_specs=[pl.BlockSpec((1, tm, ATT_W), row), pl.BlockSpec((1, tm, cw), row),
                      pl.BlockSpec((1, tm, 2 * d), row), pl.BlockSpec((1, tm, d), row),
                      pl.BlockSpec((1, mod.shape[1], d), lambda b, i, l: (mod_rows(l[0], b), 0, 2)),
                      _resident((1, ATT_W, d), lay), _resident((1, cw, d), lay), _resident((1, d, d), lay)],
            out_specs=pl.BlockSpec((1, tm, d), row)),
        out_shape=jax.ShapeDtypeStruct((nb, t, d), F32),
        input_output_aliases={4: 0},
        compiler_params=_params(("parallel", "parallel")),
        name="merge_out_proj",
    )(lidx, a, cv, gm, x, mod, w_ao, w_co, w_o)


def _rope_tables(pos, head_dim):
    rd = head_dim // 4
    half = rd // 2
    inv = ROPE_THETA ** (-jnp.arange(half, dtype=F32) / half)
    ang = pos.astype(F32)[:, None] * inv[None, :]
    cos, sin = jnp.cos(ang), jnp.sin(ang)
    n = pos.shape[0]
    pad = lambda w: jnp.zeros((n, w), F32)
    c = jnp.concatenate([cos, cos, jnp.ones((n, head_dim - rd), F32)], axis=-1)
    s1 = jnp.concatenate([-sin, pad(head_dim - half)], axis=-1)
    s2 = jnp.concatenate([pad(half), sin, pad(head_dim - rd)], axis=-1)
    rep = LANE // head_dim
    return tuple(jnp.tile(a, (1, rep)) for a in (c, s1, s2))


def _pad_rows(a, rows, axis):
    pad = [(0, 0)] * a.ndim
    pad[axis] = (0, rows - a.shape[axis])
    return jnp.pad(a, pad)


def kernel(x_prompt, x_sample, c_prompt, c_sample, cache_k, cache_v, cache_kidx, state_conv, page_table, norm_g, w_ada, b_ada, w_in, q_norm_g, k_norm_g, idx_k_norm_g, conv_w, conv_b, conv_ln_g, conv_ln_b, w_attn_out, w_conv_out, w_out):
    batch, seq, d = x_prompt.shape
    dec_batch, dec_seq, _ = x_sample.shape
    n_layers, n_phys, page = cache_k.shape[:3]
    n_pages = page_table.shape[1]
    past_len = n_pages * page
    cw = conv_w.shape[-1]
    topk_p = min(TOPK_MAX, seq // 4)
    topk_s = min(TOPK_MAX, (past_len + dec_seq) // 4)
    n_dec = dec_batch * dec_seq
    assert dec_seq <= SAMPLE_ROWS and CONV_K - 1 <= CONV_HALO

    o_ki = ATT_W + 2 * KV_W + QI_W
    o_ga = o_ki + IDX_DIM + N_IDX_HEADS
    o_glu = o_ga + ATT_W
    o_mg = o_glu + 3 * cw
    w_in16 = w_in.astype(BF16)
    w_a = w_in16[..., :o_ki]
    w_kw = _pad_rows(w_in16[..., o_ki:o_ga], LANE, 2)
    w_ga = w_in16[..., o_ga:o_glu]
    w_b = w_in16[..., o_glu:o_mg]
    w_mg = w_in16[..., o_mg:]
    w_ao, w_co, w_o = w_attn_out.astype(BF16), w_conv_out.astype(BF16), w_out.astype(BF16)
    qg = q_norm_g.reshape(n_layers, 1, HEAD_DIM)
    kg = k_norm_g.reshape(n_layers, 1, HEAD_DIM)
    gi = _pad_rows(idx_k_norm_g, LANE, 1).reshape(n_layers, 1, LANE)
    conv_w_p = _pad_rows(conv_w, CONV_HALO, 1)
    as_row = lambda a: a.reshape(n_layers, 1, a.shape[-1])

    n_c = batch + dec_batch
    c_all = _pad_rows(jnp.concatenate([c_prompt, c_sample], axis=0), -(-n_c // 8) * 8, 0)
    mod = _modulation(c_all, w_ada, b_ada)
    mod_p = mod[:, :batch].reshape(n_layers * batch, 1, 3 * d)
    mod_s = jnp.repeat(mod[:, batch:n_c], dec_seq, axis=1)
    mod_rows_p = lambda l, b: l * batch + b
    mod_rows_s = lambda l, b: l

    rope_p = _rope_tables(jnp.arange(seq, dtype=jnp.int32), HEAD_DIM)
    idx_p = _rope_tables(jnp.arange(seq, dtype=jnp.int32), IDX_DIM)
    pos_s = jnp.tile(past_len + jnp.arange(dec_seq, dtype=jnp.int32), dec_batch)
    rope_s = _rope_tables(pos_s, HEAD_DIM)
    idx_s = _rope_tables(pos_s, IDX_DIM)

    cache_k2 = cache_k.reshape(n_layers * n_phys, page * N_KV_HEADS, HEAD_DIM)
    cache_v2 = cache_v.reshape(n_layers * n_phys, page * N_KV_HEADS, HEAD_DIM)
    cache_ki2 = cache_kidx.reshape(n_layers * n_phys, page, IDX_DIM)
    state_p = jnp.zeros((1, CONV_HALO, cw), F32)
    state_s = jnp.pad(state_conv, ((0, 0), (0, 0), (CONV_HALO - (CONV_K - 1), 0), (0, 0)))
    state_s = state_s.reshape(n_layers * dec_batch, CONV_HALO, cw)

    tm = min(512, seq)
    tq = min(256, seq)
    ck = tm
    tt = min(256, seq)
    pages = min(16, n_pages)

    def to_sample_rows(a, axis):
        shp = a.shape[:axis] + (dec_batch, dec_seq) + a.shape[axis + 1:]
        a = jnp.moveaxis(a.reshape(shp), axis, 0)
        return _pad_rows(a, SAMPLE_ROWS, axis + 1)

    def layer(carry, l):
        xp, xs = carry
        lidx = jnp.full((1,), l, jnp.int32)

        h = _norm(lidx, xp, as_row(norm_g), mod_p, mod_rows_p, tm)
        q, k32, kbf, v32, vt, qi, ki32, kibf, wi = _attn_in(lidx, h, w_a, w_kw, qg, kg, gi, rope_p, idx_p, tm, True)
        u, sgc = _conv_in(lidx, h, w_b, tm)
        sga, gm = _gate_in(lidx, h, w_ga, w_mg, tm)
        a = _prompt_attention(q, qi, wi, kbf, vt, kibf, sga, topk_p, tq, ck)
        cv = _conv(lidx, u, state_p, lambda l_, b: 0, conv_w_p, as_row(conv_b), as_row(conv_ln_g),
                   as_row(conv_ln_b), sgc, tt)
        xp_new = _out(lidx, a, cv, gm, xp, mod_p, mod_rows_p, w_ao, w_co, w_o, tm)
        outs_p = (k32.reshape(batch, seq, N_KV_HEADS, HEAD_DIM), v32.reshape(batch, seq, N_KV_HEADS, HEAD_DIM),
                  ki32, u[:, seq - (CONV_K - 1):])

        xs2 = xs.reshape(1, n_dec, d)
        h = _norm(lidx, xs2, as_row(norm_g), mod_s, mod_rows_s, n_dec)
        q, k32, kbf, v32, vbf, qi, ki32, kibf, wi = _attn_in(lidx, h, w_a, w_kw, qg, kg, gi, rope_s, idx_s, n_dec,
                                                             False)
        u, sgc = _conv_in(lidx, h, w_b, n_dec)
        sga, gm = _gate_in(lidx, h, w_ga, w_mg, n_dec)
        q_r = to_sample_rows(q[0], 1)
        qi_r = to_sample_rows(qi[0], 1)
        wi_r = to_sample_rows(wi[0], 0)
        sga_r = to_sample_rows(sga[0], 0)
        new_slots = lambda a_: _pad_rows(a_[0].reshape(dec_batch, dec_seq, a_.shape[-1]), LANE, 1)
        scores = _sample_scores(lidx, page_table, qi_r, wi_r, cache_ki2, n_phys, pages)
        thr, newsc, jcut = _sample_threshold(scores, qi_r, wi_r, new_slots(kibf), topk_s, dec_seq)
        a = _sample_attention(lidx, page_table, q_r, scores, thr, newsc, jcut, new_slots(kbf), new_slots(vbf),
                              sga_r, cache_k2, cache_v2, n_phys, pages)
        a = a[:, :dec_seq].reshape(1, n_dec, ATT_W)
        u_r = u[0].reshape(dec_batch, dec_seq, cw)
        cv = _conv(lidx, _pad_rows(u_r, SAMPLE_ROWS, 1), state_s, lambda l_, b: l_ * dec_batch + b, conv_w_p,
                   as_row(conv_b), as_row(conv_ln_g), as_row(conv_ln_b),
                   _pad_rows(sgc[0].reshape(dec_batch, dec_seq, cw), SAMPLE_ROWS, 1), SAMPLE_ROWS)
        cv = cv[:, :dec_seq].reshape(1, n_dec, cw)
        xs_new = _out(lidx, a, cv, gm, xs2, mod_s, mod_rows_s, w_ao, w_co, w_o, n_dec).reshape(xs.shape)
        new_conv_s = jnp.concatenate([state_conv[l], u_r], axis=1)[:, -(CONV_K - 1):]
        outs_s = (k32.reshape(dec_batch, dec_seq, N_KV_HEADS, HEAD_DIM),
                  v32.reshape(dec_batch, dec_seq, N_KV_HEADS, HEAD_DIM),
                  ki32.reshape(dec_batch, dec_seq, IDX_DIM), new_conv_s)
        return (xp_new, xs_new), outs_p + outs_s

    carry, per_layer = (x_prompt, x_sample), []
    for l in range(n_layers):
        carry, outs = layer(carry, l)
        per_layer.append(outs)
    return carry + tuple(jnp.stack(leaf) for leaf in zip(*per_layer))
```

```python
import functools

import jax
import jax.numpy as jnp
from jax import lax
from jax.experimental import pallas as pl
from jax.experimental.pallas import tpu as pltpu

F32 = jnp.float32
BF16 = jnp.bfloat16
I32 = jnp.int32

N_HEADS = 8
HEAD_DIM = 128
N_KV_HEADS = 2
HEADS_PER_KV = N_HEADS // N_KV_HEADS
N_IDX_HEADS = 16
IDX_DIM = 64
TOPK_MAX = 256
CONV_K = 31
ROPE_THETA = 500000.0
EPS = 1e-6

LANE = 128
ATT_W = N_HEADS * HEAD_DIM
KV_W = N_KV_HEADS * HEAD_DIM
QI_W = N_IDX_HEADS * IDX_DIM
CONV_HALO = 32
SAMPLE_ROWS = 16
KEY_NEG_INF = -2139095041
KEY_POS_INF = 0x7F800000
VALUE_PASSES = 24
PASSES_PER_TRIP = 4
MAX_PASSES = VALUE_PASSES + 32
NEG_BIG = -1e30
LOG2E = 1.4426950408889634
VMEM_LIMIT = 56 * 1024 * 1024

_NT = (((1,), (1,)), ((), ()))


def _params(sem):
    return pltpu.CompilerParams(dimension_semantics=sem, vmem_limit_bytes=VMEM_LIMIT)


def _resident(block, imap):
    return pl.BlockSpec(block, imap, pipeline_mode=pl.Buffered(1))


def _silu(x):
    return x * jax.nn.sigmoid(x)


def _mod_kernel(c_ref, w_ref, b_ref, o_ref):
    a = _silu(c_ref[...]).astype(BF16)
    o_ref[0] = jnp.dot(a, w_ref[0].astype(BF16), preferred_element_type=F32) + b_ref[0]


def _modulation(c_all, w_ada, b_ada):
    n_layers, d, n3 = w_ada.shape
    rows = c_all.shape[0]
    tn = next(w for w in (512, 256, LANE, n3) if n3 % w == 0)
    return pl.pallas_call(
        _mod_kernel,
        grid=(n_layers, n3 // tn),
        in_specs=[pl.BlockSpec((rows, d), lambda l, j: (0, 0)),
                  pl.BlockSpec((1, d, tn), lambda l, j: (l, 0, j)),
                  pl.BlockSpec((1, 1, tn), lambda l, j: (l, 0, j))],
        out_specs=pl.BlockSpec((1, rows, tn), lambda l, j: (l, 0, j)),
        out_shape=jax.ShapeDtypeStruct((n_layers, rows, n3), F32),
        compiler_params=_params(("parallel", "parallel")),
        name="adaln_mod",
    )(c_all, w_ada, b_ada.reshape(n_layers, 1, n3))


def _norm_kernel(l_ref, x_ref, g_ref, sc_ref, sh_ref, o_ref):
    x = x_ref[0]
    y = x * lax.rsqrt(jnp.mean(x * x, axis=-1, keepdims=True) + EPS) * g_ref[0]
    o_ref[0] = (y * (1.0 + sc_ref[0]) + sh_ref[0]).astype(BF16)


def _norm(lidx, x, norm_g, mod, mod_rows, tm):
    nb, t, d = x.shape
    mod_imap = lambda part: (lambda b, i, l: (mod_rows(l[0], b), 0, part))
    return pl.pallas_call(
        _norm_kernel,
        grid_spec=pltpu.PrefetchScalarGridSpec(
            num_scalar_prefetch=1, grid=(nb, t // tm),
            in_specs=[pl.BlockSpec((1, tm, d), lambda b, i, l: (b, i, 0)),
                      pl.BlockSpec((1, 1, d), lambda b, i, l: (l[0], 0, 0)),
                      pl.BlockSpec((1, mod.shape[1], d), mod_imap(1)),
                      pl.BlockSpec((1, mod.shape[1], d), mod_imap(0))],
            out_specs=pl.BlockSpec((1, tm, d), lambda b, i, l: (b, i, 0))),
        out_shape=jax.ShapeDtypeStruct((nb, t, d), BF16),
        compiler_params=_params(("parallel", "parallel")),
        name="adaln_rmsnorm",
    )(lidx, x, norm_g, mod, mod)


def _rope(n, c, s1, s2, half):
    return n * c + pltpu.roll(n, LANE - half, 1) * s1 + pltpu.roll(n, half, 1) * s2


def _attn_in_kernel(l_ref, h_ref, w_ref, wkw_ref, qg_ref, kg_ref, gi_ref, rc_ref, rs1_ref, rs2_ref,
                    ic_ref, is1_ref, is2_ref, *refs, v_transposed, stacked):
    if stacked:
        refs = refs[3:]
    q_ref, k32_ref, kbf_ref, v32_ref, vbf_ref, qi_ref, ki32_ref, kibf_ref, wi_ref = refs
    h = h_ref[0]
    tm = h.shape[0]
    rc, rs1, rs2 = rc_ref[...], rs1_ref[...], rs2_ref[...]
    ic, is1, is2 = ic_ref[...], is1_ref[...], is2_ref[...]

    def put_head(ref, g, val):
        if stacked:
            ref[0, 0, pl.ds(g, tm, stride=N_KV_HEADS), :] = val
        else:
            ref[0, :, g * HEAD_DIM:(g + 1) * HEAD_DIM] = val

    def proj(c0, width):
        return jnp.dot(h, w_ref[0, :, c0:c0 + width], preferred_element_type=F32)

    def head_norm(z, g):
        return z * lax.rsqrt(jnp.mean(z * z, axis=-1, keepdims=True) + EPS) * g

    scale = HEAD_DIM ** -0.5 * LOG2E
    for pair in range(N_HEADS // 2):
        z2 = proj(pair * 2 * HEAD_DIM, 2 * HEAD_DIM)
        for j in range(2):
            z = z2[:, j * HEAD_DIM:(j + 1) * HEAD_DIM]
            r = _rope(head_norm(z, qg_ref[0]), rc, rs1, rs2, HEAD_DIM // 8)
            q_ref[0, 2 * pair + j] = (r * scale).astype(BF16)
    z2 = proj(ATT_W, KV_W)
    for g in range(N_KV_HEADS):
        z = z2[:, g * HEAD_DIM:(g + 1) * HEAD_DIM]
        r = _rope(head_norm(z, kg_ref[0]), rc, rs1, rs2, HEAD_DIM // 8)
        put_head(k32_ref, g, r)
        kbf_ref[0, :, g * HEAD_DIM:(g + 1) * HEAD_DIM] = r.astype(BF16)
    z2 = proj(ATT_W + KV_W, KV_W)
    for g in range(N_KV_HEADS):
        put_head(v32_ref, g, z2[:, g * HEAD_DIM:(g + 1) * HEAD_DIM])
    if v_transposed:
        vbf_ref[0, 0] = z2.T.astype(BF16)
    else:
        vbf_ref[0] = z2.astype(BF16)

    lane = lax.broadcasted_iota(I32, (1, LANE), 1)
    lo = (lane < IDX_DIM).astype(F32)
    o_qi = ATT_W + 2 * KV_W
    for quad in range(N_IDX_HEADS // 4):
        z4 = proj(o_qi + quad * 2 * LANE, 2 * LANE)
        for j in range(2):
            r = _rope(z4[:, j * LANE:(j + 1) * LANE], ic, is1, is2, IDX_DIM // 8)
            qi_ref[0, 4 * quad + 2 * j] = (r * lo).astype(BF16)
            qi_ref[0, 4 * quad + 2 * j + 1] = (pltpu.roll(r, IDX_DIM, 1) * lo).astype(BF16)
    z = jnp.dot(h, wkw_ref[0], preferred_element_type=F32)
    ss = jnp.sum(z * z * lo, axis=-1, keepdims=True) * (1.0 / IDX_DIM)
    n = z * lax.rsqrt(ss + EPS) * gi_ref[0]
    r = _rope(n, ic, is1, is2, IDX_DIM // 8)
    if stacked:
        ki32_ref[0, 0] = r[:, :IDX_DIM]
    else:
        ki32_ref[0] = r[:, :IDX_DIM]
    kibf_ref[0] = r.astype(BF16)
    wi_ref[0] = pltpu.roll(z, LANE - IDX_DIM, 1) * (N_IDX_HEADS ** -0.5 * IDX_DIM ** -0.5)


def _attn_in(lidx, h, w_a, w_kw, qg, kg, gi, rope_tabs, idx_tabs, tm, v_transposed, stacked=None):
    nb, t, d = h.shape
    na = w_a.shape[-1]
    row = lambda b, i, l: (b, i, 0)
    hm = lambda b, i, l: (b, 0, i, 0)
    lay = lambda b, i, l: (l[0], 0, 0)
    tab = pl.BlockSpec((tm, LANE), lambda b, i, l: (i, 0))
    if v_transposed:
        v_out = (jax.ShapeDtypeStruct((nb, t // tm, KV_W, tm), BF16),
                 pl.BlockSpec((1, 1, KV_W, tm), lambda b, i, l: (b, i, 0, 0)))
    else:
        v_out = (jax.ShapeDtypeStruct((nb, t, KV_W), BF16), pl.BlockSpec((1, tm, KV_W), row))
    outs = [
        (jax.ShapeDtypeStruct((nb, N_HEADS, t, HEAD_DIM), BF16), pl.BlockSpec((1, N_HEADS, tm, HEAD_DIM), hm)),
        (jax.ShapeDtypeStruct((nb, t, KV_W), F32), pl.BlockSpec((1, tm, KV_W), row)),
        (jax.ShapeDtypeStruct((nb, t, KV_W), BF16), pl.BlockSpec((1, tm, KV_W), row)),
        (jax.ShapeDtypeStruct((nb, t, KV_W), F32), pl.BlockSpec((1, tm, KV_W), row)),
        v_out,
        (jax.ShapeDtypeStruct((nb, N_IDX_HEADS, t, LANE), BF16), pl.BlockSpec((1, N_IDX_HEADS, tm, LANE), hm)),
        (jax.ShapeDtypeStruct((nb, t, IDX_DIM), F32), pl.BlockSpec((1, tm, IDX_DIM), row)),
        (jax.ShapeDtypeStruct((nb, t, LANE), BF16), pl.BlockSpec((1, tm, LANE), row)),
        (jax.ShapeDtypeStruct((nb, t, LANE), F32), pl.BlockSpec((1, tm, LANE), row)),
    ]
    in_specs = [pl.BlockSpec((1, tm, d), row),
                _resident((1, d, na), lay), _resident((1, d, LANE), lay),
                pl.BlockSpec((1, 1, LANE), lay), pl.BlockSpec((1, 1, LANE), lay),
                pl.BlockSpec((1, 1, LANE), lay),
                tab, tab, tab, tab, tab, tab]
    args = [lidx, h, w_a, w_kw, qg, kg, gi, *rope_tabs, *idx_tabs]
    aliases = {}
    if stacked is not None:
        layer_rows = lambda b, i, l: (l[0], b, i, 0)
        for out_idx, buf, rows in zip((1, 3, 6), stacked, (N_KV_HEADS * tm, N_KV_HEADS * tm, tm)):
            outs[out_idx] = (jax.ShapeDtypeStruct(buf.shape, buf.dtype),
                             pl.BlockSpec((1, 1, rows, buf.shape[-1]), layer_rows))
            aliases[len(args)] = out_idx
            in_specs.append(pl.BlockSpec(memory_space=pl.ANY))
            args.append(buf)
    return pl.pallas_call(
        functools.partial(_attn_in_kernel, v_transposed=v_transposed, stacked=stacked is not None),
        grid_spec=pltpu.PrefetchScalarGridSpec(
            num_scalar_prefetch=1, grid=(nb, t // tm), in_specs=in_specs, out_specs=[o[1] for o in outs]),
        out_shape=[o[0] for o in outs],
        input_output_aliases=aliases,
        compiler_params=_params(("parallel", "parallel")),
        name="attn_in_proj",
    )(*args)


def _conv_in_kernel(l_ref, h_ref, w_ref, u_ref, sgc_ref, *, cw, bw):
    h = h_ref[0]
    for c0 in range(0, cw, bw):
        za = jnp.dot(h, w_ref[0, :, c0:c0 + bw], preferred_element_type=F32)
        zb = jnp.dot(h, w_ref[0, :, cw + c0:cw + c0 + bw], preferred_element_type=F32)
        u_ref[0, :, c0:c0 + bw] = za * jax.nn.sigmoid(zb)
        zg = jnp.dot(h, w_ref[0, :, 2 * cw + c0:2 * cw + c0 + bw], preferred_element_type=F32)
        sgc_ref[0, :, c0:c0 + bw] = _silu(zg).astype(BF16)


def _conv_in(lidx, h, w_b, tm):
    nb, t, d = h.shape
    cw = w_b.shape[-1] // 3
    row = lambda b, i, l: (b, i, 0)
    return pl.pallas_call(
        functools.partial(_conv_in_kernel, cw=cw, bw=min(256, cw)),
        grid_spec=pltpu.PrefetchScalarGridSpec(
            num_scalar_prefetch=1, grid=(nb, t // tm),
            in_specs=[pl.BlockSpec((1, tm, d), row),
                      _resident((1, d, 3 * cw), lambda b, i, l: (l[0], 0, 0))],
            out_specs=[pl.BlockSpec((1, tm, cw), row), pl.BlockSpec((1, tm, cw), row)]),
        out_shape=[jax.ShapeDtypeStruct((nb, t, cw), F32), jax.ShapeDtypeStruct((nb, t, cw), BF16)],
        compiler_params=_params(("parallel", "parallel")),
        name="conv_in_proj",
    )(lidx, h, w_b)


def _gate_in_kernel(l_ref, h_ref, wga_ref, wmg_ref, sga_ref, gm_ref, *, bw):
    h = h_ref[0]
    for c0 in range(0, ATT_W, bw):
        z = jnp.dot(h, wga_ref[0, :, c0:c0 + bw], preferred_element_type=F32)
        sga_ref[0, :, c0:c0 + bw] = _silu(z).astype(BF16)
    for c0 in range(0, gm_ref.shape[-1], bw):
        z = jnp.dot(h, wmg_ref[0, :, c0:c0 + bw], preferred_element_type=F32)
        gm_ref[0, :, c0:c0 + bw] = jax.nn.sigmoid(z).astype(BF16)


def _gate_in(lidx, h, w_ga, w_mg, tm):
    nb, t, d = h.shape
    nm = w_mg.shape[-1]
    row = lambda b, i, l: (b, i, 0)
    lay = lambda b, i, l: (l[0], 0, 0)
    return pl.pallas_call(
        functools.partial(_gate_in_kernel, bw=256),
        grid_spec=pltpu.PrefetchScalarGridSpec(
            num_scalar_prefetch=1, grid=(nb, t // tm),
            in_specs=[pl.BlockSpec((1, tm, d), row), _resident((1, d, ATT_W), lay), _resident((1, d, nm), lay)],
            out_specs=[pl.BlockSpec((1, tm, ATT_W), row), pl.BlockSpec((1, tm, nm), row)]),
        out_shape=[jax.ShapeDtypeStruct((nb, t, ATT_W), BF16), jax.ShapeDtypeStruct((nb, t, nm), BF16)],
        compiler_params=_params(("parallel", "parallel")),
        name="gate_in_proj",
    )(lidx, h, w_ga, w_mg)


def _key_to_float(key):
    b = key ^ ((key >> 31) & jnp.int32(0x7FFFFFFF))
    return lax.bitcast_convert_type(b, F32)


def _float_to_key(x):
    b = lax.bitcast_convert_type(x, I32)
    return b ^ ((b >> 31) & jnp.int32(0x7FFFFFFF))


def _kth_largest(count_ge, lo_f, hi_f, n_ge_lo, k):
    kf = jnp.float32(k)
    select_all = n_ge_lo <= kf
    lo0 = jnp.where(select_all, jnp.int32(KEY_NEG_INF + 1), _float_to_key(lo_f))
    hi0 = _float_to_key(hi_f) + 1

    def one_pass(it, lo, hi, clo, active):
        lo_v = _key_to_float(lo)
        hi_v = _key_to_float(jnp.minimum(hi, jnp.int32(KEY_POS_INF)))
        mid_value = _float_to_key(0.5 * lo_v + 0.5 * hi_v)
        mid_image = (lo & hi) + ((lo ^ hi) >> 1)
        cand = jnp.where(it < VALUE_PASSES, mid_value, mid_image)
        cand = jnp.minimum(jnp.maximum(cand, lo + 1), hi - 1)
        cnt = count_ge(_key_to_float(cand))
        up = (cnt >= kf) & (active > 0.0)
        down = (cnt < kf) & (active > 0.0)
        lo = jnp.where(up, cand, lo)
        clo = jnp.where(up, cnt, clo)
        hi = jnp.where(down, cand, hi)
        active = jnp.where((clo == kf) | (hi <= lo + 1), 0.0, active)
        return lo, hi, clo, active

    def cond(st):
        return (st[0] < MAX_PASSES) & (jnp.sum(st[4]) > 0.0)

    def body(st):
        it, lo, hi, clo, active = st
        for sub in range(PASSES_PER_TRIP):
            lo, hi, clo, active = one_pass(it + sub, lo, hi, clo, active)
        return it + PASSES_PER_TRIP, lo, hi, clo, active

    active0 = jnp.where(select_all, 0.0, 1.0)
    st = lax.while_loop(cond, body, (jnp.int32(0), lo0, hi0, n_ge_lo, active0))
    thr = _key_to_float(jnp.maximum(st[1], jnp.int32(KEY_NEG_INF + 1)))
    return thr, st[3] - kf


def _tie_cutoff(tie_count, surplus, n_keys):
    keep = tie_count(jnp.full(surplus.shape, n_keys, I32)) - surplus

    def body(_, st):
        lo, hi = st
        mid = lo + ((hi - lo) >> 1)
        ok = tie_count(mid) >= keep
        return jnp.where(ok, lo, mid), jnp.where(ok, mid, hi)

    lo0 = jnp.full(surplus.shape, -1, I32)
    hi0 = jnp.full(surplus.shape, n_keys - 1, I32)
    _, hi = lax.fori_loop(0, max(n_keys - 1, 1).bit_length() + 1, body, (lo0, hi0))
    return jnp.where(surplus > 0.0, hi, jnp.int32(n_keys))


def _fold_rows(x, op):
    r, n = x.shape
    if r > 64:
        x = op(x.reshape(r // 64, 64, n), axis=0)
    return op(x.reshape(x.shape[0] // 8, 8, n), axis=0)


def _index_scores(qi, wi, kc, rows):
    sc = jnp.zeros((rows, kc.shape[0]), F32)
    for hg in range(N_IDX_HEADS // 4):
        d = lax.dot_general(qi[hg * 4 * rows:(hg + 1) * 4 * rows], kc, _NT, preferred_element_type=F32)
        for j in range(4):
            hd = hg * 4 + j
            sc = sc + jnp.maximum(d[j * rows:(j + 1) * rows], 0.0) * wi[:, hd:hd + 1]
    return sc


def _softmax_step(s, vc, mask, m_ref, l_ref, acc_ref, rows):
    ck = vc.shape[0]
    s = jnp.where(mask[None], s.reshape(HEADS_PER_KV, rows, ck), NEG_BIG).reshape(HEADS_PER_KV * rows, ck)
    m_prev = m_ref[:, :1]
    m_new = jnp.maximum(m_prev, jnp.max(s, axis=-1, keepdims=True))
    alpha = jnp.exp2(m_prev - m_new)
    p = jnp.exp2(s - m_new)
    l_ref[...] = jnp.broadcast_to(alpha * l_ref[:, :1] + jnp.sum(p, axis=-1, keepdims=True), l_ref.shape)
    acc_ref[...] = alpha * acc_ref[...] + jnp.dot(p.astype(BF16), vc, preferred_element_type=F32)
    m_ref[...] = jnp.broadcast_to(m_new, m_ref.shape)


def _softmax_init(m_ref, l_ref, acc_ref):
    m_ref[...] = jnp.full(m_ref.shape, NEG_BIG, F32)
    l_ref[...] = jnp.zeros(l_ref.shape, F32)
    acc_ref[...] = jnp.zeros(acc_ref.shape, F32)


def _softmax_finish(g, sga_ref, o_ref, l_ref, acc_ref, rows):
    o = acc_ref[...] / l_ref[:, :1]
    for j in range(HEADS_PER_KV):
        c0 = (g * HEADS_PER_KV + j) * HEAD_DIM
        gate = sga_ref[0, :, c0:c0 + HEAD_DIM].astype(F32)
        o_ref[0, :, c0:c0 + HEAD_DIM] = (o[j * rows:(j + 1) * rows] * gate).astype(BF16)


def _prompt_attn_kernel(q_ref, qi_ref, wi_ref, k_ref, vt_ref, ki_ref, sga_ref, o_ref,
                        sc_s, m_s, l_s, acc_s, *, tq, ck, topk):
    row0 = pl.program_id(1) * tq
    nch = (row0 + tq + ck - 1) // ck
    wit = wi_ref[0].T
    kpos = lax.broadcasted_iota(I32, (ck, tq), 0)
    qpos = row0 + lax.broadcasted_iota(I32, (ck, tq), 1)

    def score_chunk(c, carry):
        kc = ki_ref[0, pl.ds(pl.multiple_of(c * ck, ck), ck), :]
        sc = jnp.zeros((ck, tq), F32)
        for hd in range(N_IDX_HEADS):
            d = lax.dot_general(kc, qi_ref[0, hd], _NT, preferred_element_type=F32)
            sc = sc + jnp.maximum(d, 0.0) * wit[hd:hd + 1, :]
        sc_s[c] = jnp.where(kpos + c * ck <= qpos, sc, -jnp.inf)
        return carry

    lax.fori_loop(0, nch, score_chunk, 0)

    def count_ge(thr):
        def body(c, acc):
            return acc + _fold_rows(jnp.where(sc_s[c] >= thr, 1.0, 0.0), jnp.sum)

        acc = lax.fori_loop(0, nch, body, jnp.zeros((8, tq), F32))
        return jnp.sum(acc, axis=0, keepdims=True)

    def fold_minmax(s, carry, masked):
        mx, mn = carry
        mx = jnp.maximum(mx, _fold_rows(s, jnp.max))
        if masked:
            s = jnp.where(s == -jnp.inf, jnp.inf, s)
        return mx, jnp.minimum(mn, _fold_rows(s, jnp.min))

    init = (jnp.full((8, tq), -jnp.inf, F32), jnp.full((8, tq), jnp.inf, F32))
    mm = lax.fori_loop(0, nch - 1, lambda c, carry: fold_minmax(sc_s[c], carry, False), init)
    mx, mn = fold_minmax(sc_s[nch - 1], mm, True)
    n_visible = (qpos[:1] + 1).astype(F32)
    thr, surplus = _kth_largest(count_ge, jnp.min(mn, axis=0, keepdims=True), jnp.max(mx, axis=0, keepdims=True),
                                n_visible, topk)

    @pl.when(jnp.max(surplus) > 0.0)
    def _():
        def tie_count(jcut):
            def body(c, acc):
                hit = (sc_s[c] == thr) & (kpos + c * ck <= jcut)
                return acc + _fold_rows(jnp.where(hit, 1.0, 0.0), jnp.sum)

            acc = lax.fori_loop(0, nch, body, jnp.zeros((8, tq), F32))
            return jnp.sum(acc, axis=0, keepdims=True)

        jcut = _tie_cutoff(tie_count, surplus, sc_s.shape[0] * ck)

        def hide(c, carry):
            s = sc_s[c]
            sc_s[c] = jnp.where((s == thr) & (kpos + c * ck > jcut), -jnp.inf, s)
            return carry

        lax.fori_loop(0, nch, hide, 0)

    _softmax_init(m_s, l_s, acc_s)

    def attend_chunk(c, carry):
        bias = jnp.where(sc_s[c] >= thr, 0.0, NEG_BIG)
        off = pl.multiple_of(c * ck, ck)

        def qk(g):
            kc = k_ref[0, pl.ds(off, ck), g * HEAD_DIM:(g + 1) * HEAD_DIM]
            qg = q_ref[0, g * HEADS_PER_KV:(g + 1) * HEADS_PER_KV].reshape(HEADS_PER_KV * tq, HEAD_DIM)
            return lax.dot_general(kc, qg, _NT, preferred_element_type=F32)

        logits = [qk(g) for g in range(N_KV_HEADS)]
        for g in range(N_KV_HEADS):
            ps, alphas = [], []
            for j in range(HEADS_PER_KV):
                hd = g * HEADS_PER_KV + j
                s = logits[g][:, j * tq:(j + 1) * tq] + bias
                m_prev = m_s[hd]
                m_new = jnp.maximum(m_prev, jnp.max(_fold_rows(s, jnp.max), axis=0, keepdims=True))
                alpha = jnp.exp2(m_prev - m_new)
                p = jnp.exp2(s - m_new)
                l_s[hd] = alpha * l_s[hd] + jnp.sum(_fold_rows(p, jnp.sum), axis=0, keepdims=True)
                m_s[hd] = m_new
                ps.append(p.astype(BF16))
                alphas.append(alpha)
            vtc = vt_ref[0, c, g * HEAD_DIM:(g + 1) * HEAD_DIM, :]
            pv = jnp.dot(vtc, jnp.concatenate(ps, axis=1), preferred_element_type=F32)
            acc_s[g] = jnp.concatenate(alphas, axis=1) * acc_s[g] + pv
        return carry

    lax.fori_loop(0, nch, attend_chunk, 0)
    for hd in range(N_HEADS):
        g, j = divmod(hd, HEADS_PER_KV)
        o = (acc_s[g, :, j * tq:(j + 1) * tq] / l_s[hd]).T
        gate = sga_ref[0, :, hd * HEAD_DIM:(hd + 1) * HEAD_DIM].astype(F32)
        o_ref[0, :, hd * HEAD_DIM:(hd + 1) * HEAD_DIM] = (o * gate).astype(BF16)


def _prompt_attention(q, qi, wi, kbf, vt, kibf, sga, topk, tq, ck):
    nb, _, t, _ = q.shape
    assert vt.shape == (nb, t // ck, KV_W, ck)
    full = lambda b, i: (b, 0, 0)
    row = lambda b, i: (b, i, 0)
    hm = lambda b, i: (b, 0, i, 0)
    return pl.pallas_call(
        functools.partial(_prompt_attn_kernel, tq=tq, ck=ck, topk=topk),
        grid=(nb, t // tq),
        in_specs=[pl.BlockSpec((1, N_HEADS, tq, HEAD_DIM), hm),
                  pl.BlockSpec((1, N_IDX_HEADS, tq, LANE), hm),
                  pl.BlockSpec((1, tq, LANE), row),
                  pl.BlockSpec((1, t, KV_W), full),
                  pl.BlockSpec((1, t // ck, KV_W, ck), lambda b, i: (b, 0, 0, 0)),
                  pl.BlockSpec((1, t, LANE), full),
                  pl.BlockSpec((1, tq, ATT_W), row)],
        out_specs=pl.BlockSpec((1, tq, ATT_W), row),
        out_shape=jax.ShapeDtypeStruct((nb, t, ATT_W), BF16),
        scratch_shapes=[pltpu.VMEM((t // ck, ck, tq), F32),
                        pltpu.VMEM((N_HEADS, 1, tq), F32),
                        pltpu.VMEM((N_HEADS, 1, tq), F32),
                        pltpu.VMEM((N_KV_HEADS, HEAD_DIM, HEADS_PER_KV * tq), F32)],
        compiler_params=_params(("parallel", "arbitrary")),
        name="prompt_sparse_attention",
    )(q, qi, wi, kbf, vt, kibf, sga)


def _paged_fetch(l_ref, pt_ref, hbm_refs, raw_refs, sems, *, pages, n_phys):
    b, j = pl.program_id(0), pl.program_id(1)
    nj = pl.num_programs(1)
    step = b * nj + j
    slot = step % 2

    def copies(bb, jj, sl):
        out = []
        for p in range(pages):
            row = l_ref[0] * n_phys + pt_ref[bb, jj * pages + p]
            for a, (hbm, raw) in enumerate(zip(hbm_refs, raw_refs)):
                n = hbm.shape[1]
                out.append(pltpu.make_async_copy(hbm.at[row], raw.at[sl, pl.ds(p * n, n)], sems.at[a, sl]))
        return out

    @pl.when(step == 0)
    def _():
        for cp in copies(b, j, slot):
            cp.start()

    @pl.when(step + 1 < pl.num_programs(0) * nj)
    def _():
        for cp in copies((step + 1) // nj, (step + 1) % nj, 1 - slot):
            cp.start()

    for cp in copies(b, j, slot):
        cp.wait()
    return slot


def _sample_scores_kernel(l_ref, pt_ref, qi_ref, wi_ref, cache_hbm, o_ref, raw, sems, *, pages, n_phys):
    slot = _paged_fetch(l_ref, pt_ref, [cache_hbm], [raw], sems, pages=pages, n_phys=n_phys)
    qi = qi_ref[0].reshape(N_IDX_HEADS * SAMPLE_ROWS, LANE)[:, :IDX_DIM]
    o_ref[0] = _index_scores(qi, wi_ref[0], raw[slot].astype(BF16), SAMPLE_ROWS)


def _sample_scores(lidx, page_table, qi, wi, cache_kidx, n_phys, pages):
    nb, n_pages = page_table.shape
    page = cache_kidx.shape[1]
    return pl.pallas_call(
        functools.partial(_sample_scores_kernel, pages=pages, n_phys=n_phys),
        grid_spec=pltpu.PrefetchScalarGridSpec(
            num_scalar_prefetch=2, grid=(nb, n_pages // pages),
            in_specs=[pl.BlockSpec((1, N_IDX_HEADS, SAMPLE_ROWS, LANE), lambda b, j, l, pt: (b, 0, 0, 0)),
                      pl.BlockSpec((1, SAMPLE_ROWS, LANE), lambda b, j, l, pt: (b, 0, 0)),
                      pl.BlockSpec(memory_space=pl.ANY)],
            out_specs=pl.BlockSpec((1, SAMPLE_ROWS, pages * page), lambda b, j, l, pt: (b, 0, j)),
            scratch_shapes=[pltpu.VMEM((2, pages * page, IDX_DIM), F32),
                            pltpu.SemaphoreType.DMA((1, 2))]),
        out_shape=jax.ShapeDtypeStruct((nb, SAMPLE_ROWS, n_pages * page), F32),
        compiler_params=_params(("arbitrary", "arbitrary")),
        name="sample_index_scores",
    )(lidx, page_table, qi, wi, cache_kidx)


def _sample_threshold_kernel(sc_ref, qi_ref, wi_ref, kin_ref, thr_ref, newsc_ref, jcut_ref, *, topk, n_new):
    rows = SAMPLE_ROWS
    n_past = sc_ref.shape[-1]
    qi = qi_ref[0].reshape(N_IDX_HEADS * rows, LANE)
    sc_new = _index_scores(qi, wi_ref[0], kin_ref[0], rows)
    r = lax.broadcasted_iota(I32, sc_new.shape, 0)
    c = lax.broadcasted_iota(I32, sc_new.shape, 1)
    sc_new = jnp.where((c <= r) & (c < n_new), sc_new, -jnp.inf)
    newsc_ref[0] = sc_new

    def count_rows(hit_new, hit_past):
        accs = [jnp.where(hit_new, 1.0, 0.0)] + [jnp.zeros((rows, LANE), F32)] * 7
        for j in range(n_past // LANE):
            accs[j % 8] = accs[j % 8] + jnp.where(hit_past(sc_ref[0, :, j * LANE:(j + 1) * LANE], j), 1.0, 0.0)
        acc = (accs[0] + accs[1]) + (accs[2] + accs[3]) + ((accs[4] + accs[5]) + (accs[6] + accs[7]))
        return jnp.sum(acc, axis=-1, keepdims=True)

    def count_ge(thr):
        tb = jnp.broadcast_to(thr, (rows, LANE))
        return count_rows(sc_new >= tb, lambda s, j: s >= tb)

    past = sc_ref[0]
    mx = jnp.maximum(jnp.max(past, axis=-1, keepdims=True), jnp.max(sc_new, axis=-1, keepdims=True))
    mn = jnp.minimum(jnp.min(past, axis=-1, keepdims=True),
                     jnp.min(jnp.where(sc_new == -jnp.inf, jnp.inf, sc_new), axis=-1, keepdims=True))
    n_visible = (n_past + jnp.minimum(r[:, :1] + 1, n_new)).astype(F32)
    thr, surplus = _kth_largest(count_ge, mn, mx, n_visible, topk)
    surplus = jnp.where(r[:, :1] < n_new, surplus, 0.0)
    thr_ref[0] = jnp.broadcast_to(thr, (rows, LANE))
    n_keys = n_past + LANE
    jcut_ref[0] = jnp.full((rows, LANE), n_keys, I32)

    @pl.when(jnp.max(surplus) > 0.0)
    def _():
        tb = jnp.broadcast_to(thr, (rows, LANE))

        def tie_count(jcut):
            jb = jnp.broadcast_to(jcut, (rows, LANE))
            return count_rows((sc_new == tb) & (n_past + c <= jb), lambda s, j: (s == tb) & (j * LANE + c <= jb))

        jcut_ref[0] = jnp.broadcast_to(_tie_cutoff(tie_count, surplus, n_keys), (rows, LANE))


def _sample_threshold(scores, qi, wi, ki_new, topk, n_new):
    nb, rows, n_keys = scores.shape
    b3 = lambda b: (b, 0, 0)
    return pl.pallas_call(
        functools.partial(_sample_threshold_kernel, topk=topk, n_new=n_new),
        grid=(nb,),
        in_specs=[pl.BlockSpec((1, rows, n_keys), b3),
                  pl.BlockSpec((1, N_IDX_HEADS, rows, LANE), lambda b: (b, 0, 0, 0)),
                  pl.BlockSpec((1, rows, LANE), b3),
                  pl.BlockSpec((1, LANE, LANE), b3)],
        out_specs=[pl.BlockSpec((1, rows, LANE), b3)] * 3,
        out_shape=[jax.ShapeDtypeStruct((nb, rows, LANE), F32), jax.ShapeDtypeStruct((nb, rows, LANE), F32),
                   jax.ShapeDtypeStruct((nb, rows, LANE), I32)],
        compiler_params=_params(("parallel",)),
        name="sample_topk_threshold",
    )(scores, qi, wi, ki_new)


def _sample_attn_kernel(l_ref, pt_ref, q_ref, sc_ref, thr_ref, newsc_ref, jcut_ref, knew_ref, vnew_ref, sga_ref,
                        ck_hbm, cv_hbm, o_ref, kraw, vraw, sems, m_s, l_s, acc_s, *, pages, n_phys):
    slot = _paged_fetch(l_ref, pt_ref, [ck_hbm, cv_hbm], [kraw, vraw], sems, pages=pages, n_phys=n_phys)
    rows = SAMPLE_ROWS
    n_chunk = kraw.shape[1] // N_KV_HEADS
    j = pl.program_id(1)
    thr = thr_ref[0][:, :1]
    jcut = jcut_ref[0][:, :1]

    @pl.when(j == 0)
    def _():
        for g in range(N_KV_HEADS):
            _softmax_init(m_s.at[g], l_s.at[g], acc_s.at[g])

    def q_group(g):
        return q_ref[0, g * HEADS_PER_KV:(g + 1) * HEADS_PER_KV].reshape(HEADS_PER_KV * rows, HEAD_DIM)

    def selected(s, first_index):
        idx = first_index + lax.broadcasted_iota(I32, s.shape, 1)
        return (s > thr) | ((s == thr) & (idx <= jcut))

    def head_rows(raw, g):
        return raw[slot, pl.ds(g, n_chunk, stride=N_KV_HEADS), :].astype(BF16)

    mask = selected(sc_ref[0], j * n_chunk)
    logits = [lax.dot_general(q_group(g), head_rows(kraw, g), _NT, preferred_element_type=F32)
              for g in range(N_KV_HEADS)]
    for g in range(N_KV_HEADS):
        _softmax_step(logits[g], head_rows(vraw, g), mask, m_s.at[g], l_s.at[g], acc_s.at[g], rows)

    @pl.when(j == pl.num_programs(1) - 1)
    def _():
        mask_new = selected(newsc_ref[0], pl.num_programs(1) * n_chunk)
        for g in range(N_KV_HEADS):
            kc = knew_ref[0, :, g * HEAD_DIM:(g + 1) * HEAD_DIM]
            vc = vnew_ref[0, :, g * HEAD_DIM:(g + 1) * HEAD_DIM]
            s = lax.dot_general(q_group(g), kc, _NT, preferred_element_type=F32)
            _softmax_step(s, vc, mask_new, m_s.at[g], l_s.at[g], acc_s.at[g], rows)
            _softmax_finish(g, sga_ref, o_ref, l_s.at[g], acc_s.at[g], rows)


def _sample_attention(lidx, page_table, q, scores, thr, newsc, jcut, k_new, v_new, sga, cache_k, cache_v, n_phys,
                      pages):
    nb, n_pages = page_table.shape
    page = cache_k.shape[1] // N_KV_HEADS
    rows = SAMPLE_ROWS
    b3 = lambda b, j, l, pt: (b, 0, 0)
    return pl.pallas_call(
        functools.partial(_sample_attn_kernel, pages=pages, n_phys=n_phys),
        grid_spec=pltpu.PrefetchScalarGridSpec(
            num_scalar_prefetch=2, grid=(nb, n_pages // pages),
            in_specs=[pl.BlockSpec((1, N_HEADS, rows, HEAD_DIM), lambda b, j, l, pt: (b, 0, 0, 0)),
                      pl.BlockSpec((1, rows, pages * page), lambda b, j, l, pt: (b, 0, j)),
                      pl.BlockSpec((1, rows, LANE), b3), pl.BlockSpec((1, rows, LANE), b3),
                      pl.BlockSpec((1, rows, LANE), b3),
                      pl.BlockSpec((1, LANE, KV_W), b3), pl.BlockSpec((1, LANE, KV_W), b3),
                      pl.BlockSpec((1, rows, ATT_W), b3),
                      pl.BlockSpec(memory_space=pl.ANY), pl.BlockSpec(memory_space=pl.ANY)],
            out_specs=pl.BlockSpec((1, rows, ATT_W), b3),
            scratch_shapes=[pltpu.VMEM((2, pages * page * N_KV_HEADS, HEAD_DIM), F32),
                            pltpu.VMEM((2, pages * page * N_KV_HEADS, HEAD_DIM), F32),
                            pltpu.SemaphoreType.DMA((2, 2)),
                            pltpu.VMEM((N_KV_HEADS, HEADS_PER_KV * rows, LANE), F32),
                            pltpu.VMEM((N_KV_HEADS, HEADS_PER_KV * rows, LANE), F32),
                            pltpu.VMEM((N_KV_HEADS, HEADS_PER_KV * rows, HEAD_DIM), F32)]),
        out_shape=jax.ShapeDtypeStruct((nb, rows, ATT_W), BF16),
        compiler_params=_params(("arbitrary", "arbitrary")),
        name="sample_sparse_attention",
    )(lidx, page_table, q, scores, thr, newsc, jcut, k_new, v_new, sga, cache_k, cache_v)


def _conv_kernel(l_ref, u_ref, st_ref, w_ref, b_ref, g_ref, beta_ref, sgc_ref, o_ref, ext_s, sh_s, d_s,
                 *, tt, rb, cb):
    cw = u_ref.shape[-1]

    @pl.when(pl.program_id(1) == 0)
    def _():
        ext_s[0:CONV_HALO] = st_ref[0]

    ext_s[CONV_HALO:CONV_HALO + tt] = u_ref[0]
    first = CONV_HALO - (CONV_K - 1)
    n_sh = sh_s.shape[1]
    for ph in range(1, 8):
        sh_s[ph - 1] = ext_s[ph:ph + n_sh, :]
    for c0 in range(0, cw, cb):
        for r0 in range(0, tt, rb):
            acc = jnp.zeros((rb, cb), F32)
            for j in range(CONV_K):
                ph, base = (first + j) % 8, (first + j) // 8 * 8 + r0
                if ph == 0:
                    x = ext_s[base:base + rb, c0:c0 + cb]
                else:
                    x = sh_s[ph - 1, base:base + rb, c0:c0 + cb]
                acc = acc + x * w_ref[0, j:j + 1, c0:c0 + cb]
            d_s[r0:r0 + rb, c0:c0 + cb] = acc
    d = d_s[...] + b_ref[0]
    mu = jnp.mean(d, axis=-1, keepdims=True)
    xc = d - mu
    y = xc * lax.rsqrt(jnp.mean(xc * xc, axis=-1, keepdims=True) + EPS) * g_ref[0] + beta_ref[0]
    o_ref[0] = (_silu(y) * sgc_ref[0].astype(F32)).astype(BF16)
    if tt >= CONV_HALO:
        ext_s[0:CONV_HALO] = ext_s[tt:tt + CONV_HALO]


def _conv(lidx, u, state, state_rows, conv_w, conv_b, ln_g, ln_b, sgc, tt):
    nb, t, cw = u.shape
    row = lambda b, i, l: (b, i, 0)
    lay = lambda b, i, l: (l[0], 0, 0)
    rb = min(32, tt)
    return pl.pallas_call(
        functools.partial(_conv_kernel, tt=tt, rb=rb, cb=min(256, cw)),
        grid_spec=pltpu.PrefetchScalarGridSpec(
            num_scalar_prefetch=1, grid=(nb, t // tt),
            in_specs=[pl.BlockSpec((1, tt, cw), row),
                      pl.BlockSpec((1, CONV_HALO, cw), lambda b, i, l: (state_rows(l[0], b), 0, 0)),
                      pl.BlockSpec((1, CONV_HALO, cw), lay),
                      pl.BlockSpec((1, 1, cw), lay), pl.BlockSpec((1, 1, cw), lay), pl.BlockSpec((1, 1, cw), lay),
                      pl.BlockSpec((1, tt, cw), row)],
            out_specs=pl.BlockSpec((1, tt, cw), row),
            scratch_shapes=[pltpu.VMEM((CONV_HALO + tt, cw), F32),
                            pltpu.VMEM((7, tt + CONV_HALO - 8, cw), F32),
                            pltpu.VMEM((tt, cw), F32)]),
        out_shape=jax.ShapeDtypeStruct((nb, t, cw), BF16),
        compiler_params=_params(("parallel", "arbitrary")),
        name="conformer_conv",
    )(lidx, u, state, conv_w, conv_b, ln_g, ln_b, sgc)


def _out_kernel(l_ref, a_ref, cv_ref, gm_ref, x_ref, gate_ref, wa_ref, wc_ref, wo_ref, o_ref):
    d = x_ref.shape[-1]
    y_att = jnp.dot(a_ref[0], wa_ref[0], preferred_element_type=F32)
    y_conv = jnp.dot(cv_ref[0], wc_ref[0], preferred_element_type=F32)
    merged = gm_ref[0, :, :d].astype(F32) * y_att + gm_ref[0, :, d:].astype(F32) * y_conv
    out = jnp.dot(merged.astype(BF16), wo_ref[0], preferred_element_type=F32)
    o_ref[0] = x_ref[0] + gate_ref[0] * out


def _out(lidx, a, cv, gm, x, mod, mod_rows, w_ao, w_co, w_o, tm):
    nb, t, d = x.shape
    cw = cv.shape[-1]
    row = lambda b, i, l: (b, i, 0)
    lay = lambda b, i, l: (l[0], 0, 0)
    return pl.pallas_call(
        _out_kernel,
        grid_spec=pltpu.PrefetchScalarGridSpec(
            num_scalar_prefetch=1, grid=(nb, t // tm),
            in_specs=[pl.BlockSpec((1, tm, ATT_W), row), pl.BlockSpec((1, tm, cw), row),
                      pl.BlockSpec((1, tm, 2 * d), row), pl.BlockSpec((1, tm, d), row),
                      pl.BlockSpec((1, mod.shape[1], d), lambda b, i, l: (mod_rows(l[0], b), 0, 2)),
                      _resident((1, ATT_W, d), lay), _resident((1, cw, d), lay), _resident((1, d, d), lay)],
            out_specs=pl.BlockSpec((1, tm, d), row)),
        out_shape=jax.ShapeDtypeStruct((nb, t, d), F32),
        input_output_aliases={4: 0},
        compiler_params=_params(("parallel", "parallel")),
        name="merge_out_proj",
    )(lidx, a, cv, gm, x, mod, w_ao, w_co, w_o)


def _rope_tables(pos, head_dim):
    rd = head_dim // 4
    half = rd // 2
    inv = ROPE_THETA ** (-jnp.arange(half, dtype=F32) / half)
    ang = pos.astype(F32)[:, None] * inv[None, :]
    cos, sin = jnp.cos(ang), jnp.sin(ang)
    n = pos.shape[0]
    pad = lambda w: jnp.zeros((n, w), F32)
    c = jnp.concatenate([cos, cos, jnp.ones((n, head_dim - rd), F32)], axis=-1)
    s1 = jnp.concatenate([-sin, pad(head_dim - half)], axis=-1)
    s2 = jnp.concatenate([pad(half), sin, pad(head_dim - rd)], axis=-1)
    rep = LANE // head_dim
    return tuple(jnp.tile(a, (1, rep)) for a in (c, s1, s2))


def _pad_rows(a, rows, axis):
    pad = [(0, 0)] * a.ndim
    pad[axis] = (0, rows - a.shape[axis])
    return jnp.pad(a, pad)


def kernel(x_prompt, x_sample, c_prompt, c_sample, cache_k, cache_v, cache_kidx, state_conv, page_table, norm_g, w_ada, b_ada, w_in, q_norm_g, k_norm_g, idx_k_norm_g, conv_w, conv_b, conv_ln_g, conv_ln_b, w_attn_out, w_conv_out, w_out):
    batch, seq, d = x_prompt.shape
    dec_batch, dec_seq, _ = x_sample.shape
    n_layers, n_phys, page = cache_k.shape[:3]
    n_pages = page_table.shape[1]
    past_len = n_pages * page
    cw = conv_w.shape[-1]
    topk_p = min(TOPK_MAX, seq // 4)
    topk_s = min(TOPK_MAX, (past_len + dec_seq) // 4)
    n_dec = dec_batch * dec_seq
    assert dec_seq <= SAMPLE_ROWS and CONV_K - 1 <= CONV_HALO

    o_ki = ATT_W + 2 * KV_W + QI_W
    o_ga = o_ki + IDX_DIM + N_IDX_HEADS
    o_glu = o_ga + ATT_W
    o_mg = o_glu + 3 * cw
    w_in16 = w_in.astype(BF16)
    w_a = w_in16[..., :o_ki]
    w_kw = _pad_rows(w_in16[..., o_ki:o_ga], LANE, 2)
    w_ga = w_in16[..., o_ga:o_glu]
    w_b = w_in16[..., o_glu:o_mg]
    w_mg = w_in16[..., o_mg:]
    w_ao, w_co, w_o = w_attn_out.astype(BF16), w_conv_out.astype(BF16), w_out.astype(BF16)
    qg = q_norm_g.reshape(n_layers, 1, HEAD_DIM)
    kg = k_norm_g.reshape(n_layers, 1, HEAD_DIM)
    gi = _pad_rows(idx_k_norm_g, LANE, 1).reshape(n_layers, 1, LANE)
    conv_w_p = _pad_rows(conv_w, CONV_HALO, 1)
    as_row = lambda a: a.reshape(n_layers, 1, a.shape[-1])

    n_c = batch + dec_batch
    c_all = _pad_rows(jnp.concatenate([c_prompt, c_sample], axis=0), -(-n_c // 8) * 8, 0)
    mod = _modulation(c_all, w_ada, b_ada)
    mod_p = mod[:, :batch].reshape(n_layers * batch, 1, 3 * d)
    mod_s = jnp.repeat(mod[:, batch:n_c], dec_seq, axis=1)
    mod_rows_p = lambda l, b: l * batch + b
    mod_rows_s = lambda l, b: l

    rope_p = _rope_tables(jnp.arange(seq, dtype=jnp.int32), HEAD_DIM)
    idx_p = _rope_tables(jnp.arange(seq, dtype=jnp.int32), IDX_DIM)
    pos_s = jnp.tile(past_len + jnp.arange(dec_seq, dtype=jnp.int32), dec_batch)
    rope_s = _rope_tables(pos_s, HEAD_DIM)
    idx_s = _rope_tables(pos_s, IDX_DIM)

    cache_k2 = cache_k.reshape(n_layers * n_phys, page * N_KV_HEADS, HEAD_DIM)
    cache_v2 = cache_v.reshape(n_layers * n_phys, page * N_KV_HEADS, HEAD_DIM)
    cache_ki2 = cache_kidx.reshape(n_layers * n_phys, page, IDX_DIM)
    state_p = jnp.zeros((1, CONV_HALO, cw), F32)
    state_s = jnp.pad(state_conv, ((0, 0), (0, 0), (CONV_HALO - (CONV_K - 1), 0), (0, 0)))
    state_s = state_s.reshape(n_layers * dec_batch, CONV_HALO, cw)

    tm = min(512, seq)
    tq = min(256, seq)
    ck = tm
    tt = min(256, seq)
    pages = min(16, n_pages)

    def to_sample_rows(a, axis):
        shp = a.shape[:axis] + (dec_batch, dec_seq) + a.shape[axis + 1:]
        a = jnp.moveaxis(a.reshape(shp), axis, 0)
        return _pad_rows(a, SAMPLE_ROWS, axis + 1)

    def layer(carry, l):
        xp, xs, kv_all = carry
        lidx = jnp.full((1,), l, jnp.int32)

        h = _norm(lidx, xp, as_row(norm_g), mod_p, mod_rows_p, tm)
        q, k_all, kbf, v_all, vt, qi, ki_all, kibf, wi = _attn_in(lidx, h, w_a, w_kw, qg, kg, gi, rope_p, idx_p, tm,
                                                                  True, stacked=kv_all)
        u, sgc = _conv_in(lidx, h, w_b, tm)
        sga, gm = _gate_in(lidx, h, w_ga, w_mg, tm)
        a = _prompt_attention(q, qi, wi, kbf, vt, kibf, sga, topk_p, tq, ck)
        cv = _conv(lidx, u, state_p, lambda l_, b: 0, conv_w_p, as_row(conv_b), as_row(conv_ln_g),
                   as_row(conv_ln_b), sgc, tt)
        xp_new = _out(lidx, a, cv, gm, xp, mod_p, mod_rows_p, w_ao, w_co, w_o, tm)
        new_conv_p = u[:, seq - (CONV_K - 1):]

        xs2 = xs.reshape(1, n_dec, d)
        h = _norm(lidx, xs2, as_row(norm_g), mod_s, mod_rows_s, n_dec)
        q, k32, kbf, v32, vbf, qi, ki32, kibf, wi = _attn_in(lidx, h, w_a, w_kw, qg, kg, gi, rope_s, idx_s, n_dec,
                                                             False)
        u, sgc = _conv_in(lidx, h, w_b, n_dec)
        sga, gm = _gate_in(lidx, h, w_ga, w_mg, n_dec)
        q_r = to_sample_rows(q[0], 1)
        qi_r = to_sample_rows(qi[0], 1)
        wi_r = to_sample_rows(wi[0], 0)
        sga_r = to_sample_rows(sga[0], 0)
        new_slots = lambda a_: _pad_rows(a_[0].reshape(dec_batch, dec_seq, a_.shape[-1]), LANE, 1)
        scores = _sample_scores(lidx, page_table, qi_r, wi_r, cache_ki2, n_phys, pages)
        thr, newsc, jcut = _sample_threshold(scores, qi_r, wi_r, new_slots(kibf), topk_s, dec_seq)
        a = _sample_attention(lidx, page_table, q_r, scores, thr, newsc, jcut, new_slots(kbf), new_slots(vbf),
                              sga_r, cache_k2, cache_v2, n_phys, pages)
        a = a[:, :dec_seq].reshape(1, n_dec, ATT_W)
        u_r = u[0].reshape(dec_batch, dec_seq, cw)
        cv = _conv(lidx, _pad_rows(u_r, SAMPLE_ROWS, 1), state_s, lambda l_, b: l_ * dec_batch + b, conv_w_p,
                   as_row(conv_b), as_row(conv_ln_g), as_row(conv_ln_b),
                   _pad_rows(sgc[0].reshape(dec_batch, dec_seq, cw), SAMPLE_ROWS, 1), SAMPLE_ROWS)
        cv = cv[:, :dec_seq].reshape(1, n_dec, cw)
        xs_new = _out(lidx, a, cv, gm, xs2, mod_s, mod_rows_s, w_ao, w_co, w_o, n_dec).reshape(xs.shape)
        new_conv_s = jnp.concatenate([state_conv[l], u_r], axis=1)[:, -(CONV_K - 1):]
        outs_s = (k32.reshape(dec_batch, dec_seq, N_KV_HEADS, HEAD_DIM),
                  v32.reshape(dec_batch, dec_seq, N_KV_HEADS, HEAD_DIM),
                  ki32.reshape(dec_batch, dec_seq, IDX_DIM), new_conv_s)
        return (xp_new, xs_new, (k_all, v_all, ki_all)), (new_conv_p,) + outs_s

    kv_all = (jnp.zeros((n_layers, batch, seq * N_KV_HEADS, HEAD_DIM), F32),
              jnp.zeros((n_layers, batch, seq * N_KV_HEADS, HEAD_DIM), F32),
              jnp.zeros((n_layers, batch, seq, IDX_DIM), F32))
    carry, per_layer = (x_prompt, x_sample, kv_all), []
    for l in range(n_layers):
        carry, outs = layer(carry, l)
        per_layer.append(outs)
    yp, ys, (k_all, v_all, ki_all) = carry
    new_conv_p, k_s, v_s, ki_s, new_conv_s = (jnp.stack(leaf) for leaf in zip(*per_layer))
    kv_shape = (n_layers, batch, seq, N_KV_HEADS, HEAD_DIM)
    return (yp, ys, k_all.reshape(kv_shape), v_all.reshape(kv_shape), ki_all, new_conv_p, k_s, v_s, ki_s, new_conv_s)
```
